```python
import jax, jax.numpy as jnp
from jax import lax
import numpy as np

D_MODEL = 1024
BATCH = 4
SEQ = 8192
DEPTH = 1

PLE_DIM = 256
EPS = 1e-6
ROPE_THETA = 10000.0
D_FF = 2816

NSA_HEADS = 8
NSA_KV_HEADS = 2
HEAD_DIM = 64
HPG = NSA_HEADS // NSA_KV_HEADS
NSA_WIDTH = NSA_HEADS * HEAD_DIM
KV_WIDTH = NSA_KV_HEADS * HEAD_DIM
CMP_LEN = 32
CMP_STRIDE = 16
CMP_HIDDEN = 256
SEL_LEN = 64
N_SEL = 16
N_LOCAL = 2
WINDOW = 512
Q_BLOCK = 128
NEG = -1e30
FORCED_SCORE = 1e4
INVALID_SCORE = -1e4

CONV_WIDTH = 512
CONV_K = 3

SPLIT_SIZES = (NSA_WIDTH, KV_WIDTH, KV_WIDTH, KV_WIDTH, KV_WIDTH, KV_WIDTH, KV_WIDTH,
               3 * NSA_HEADS, CONV_WIDTH, CONV_WIDTH, CONV_WIDTH, D_MODEL, D_MODEL)
IN_COLS = 4888

kernel_name = "hybrid_nsa_shortconv_macaron_block"


def rms_norm(x, g):
    xf = x.astype(jnp.float32)
    y = xf * lax.rsqrt(jnp.mean(xf * xf, axis=-1, keepdims=True) + EPS)
    return (y * g.astype(jnp.float32)).astype(x.dtype)


def swiglu(x, w_gate, w_up, w_down):
    return (jax.nn.silu(x @ w_gate) * (x @ w_up)) @ w_down


def rope(x, positions):
    half = HEAD_DIM // 2
    inv_freq = ROPE_THETA ** (-jnp.arange(half, dtype=jnp.float32) / half)
    ang = positions.astype(jnp.float32)[..., None] * inv_freq
    cos = jnp.cos(ang)[:, :, None, :]
    sin = jnp.sin(ang)[:, :, None, :]
    x1 = x[..., :half].astype(jnp.float32)
    x2 = x[..., half:].astype(jnp.float32)
    out = jnp.concatenate([x1 * cos - x2 * sin, x2 * cos + x1 * sin], axis=-1)
    return out.astype(x.dtype)


def compress(kv, pos_emb, w1, b1, w2):
    B, S, G, dh = kv.shape
    nc = (S - CMP_LEN) // CMP_STRIDE + 1
    idx = jnp.arange(nc)[:, None] * CMP_STRIDE + jnp.arange(CMP_LEN)[None, :]
    blocks = kv[:, idx] + pos_emb[None, None, :, None, :]
    flat = blocks.transpose(0, 1, 3, 2, 4).reshape(B, nc, G, CMP_LEN * dh)
    return jax.nn.gelu(flat @ w1 + b1) @ w2


def nsa_attention(q, kc, vc, ks, vs, kw, vw, gates):
    B, S = q.shape[0], q.shape[1]
    G = NSA_KV_HEADS
    nc = kc.shape[1]
    ns = S // SEL_LEN
    n_sel = min(N_SEL, ns)
    nqb = S // Q_BLOCK
    scale = HEAD_DIM ** -0.5
    f32 = jnp.float32

    ks_blk = ks.reshape(B, ns, SEL_LEN, G, HEAD_DIM).transpose(0, 3, 1, 2, 4)
    vs_blk = vs.reshape(B, ns, SEL_LEN, G, HEAD_DIM).transpose(0, 3, 1, 2, 4)
    pad = ((0, 0), (WINDOW, 0), (0, 0), (0, 0))
    kw_pad = jnp.pad(kw, pad)
    vw_pad = jnp.pad(vw, pad)

    c_start = jnp.arange(nc) * CMP_STRIDE
    c_last = c_start + CMP_LEN - 1
    s_start = jnp.arange(ns) * SEL_LEN
    overlap = jnp.clip(jnp.minimum(c_start[:, None] + CMP_LEN, s_start[None, :] + SEL_LEN)
                       - jnp.maximum(c_start[:, None], s_start[None, :]), 0, None).astype(f32) / CMP_LEN
    s_idx = jnp.arange(ns)
    sel_off = jnp.arange(SEL_LEN)
    win_off = jnp.arange(Q_BLOCK + WINDOW)

    def block(args):
        qb, gb, qs = args
        tq = qs + jnp.arange(Q_BLOCK)

        s_c = jnp.einsum('bqghd,bcgd->bghqc', qb, kc, preferred_element_type=f32) * scale
        c_mask = c_last[None, :] <= tq[:, None]
        p_c = jax.nn.softmax(jnp.where(c_mask, s_c, NEG), axis=-1)
        p_c = p_c * jnp.any(c_mask, axis=-1)[:, None].astype(f32)
        o_c = jnp.einsum('bghqc,bcgd->bqghd', p_c.astype(vc.dtype), vc)

        imp = jnp.einsum('bghqc,cs->bgqs', p_c, overlap)
        blk_q = tq // SEL_LEN
        valid = s_idx[None, :] <= blk_q[:, None]
        forced = (s_idx[None, :] == 0) | (valid & (s_idx[None, :] > blk_q[:, None] - N_LOCAL))
        score = jnp.where(forced, FORCED_SCORE, jnp.where(valid, imp, INVALID_SCORE))
        _, sel = lax.top_k(score, n_sel)

        gather = jax.vmap(jax.vmap(lambda kb, ix: kb[ix]))
        gk = gather(ks_blk, sel)
        gv = gather(vs_blk, sel)
        tok = sel[..., None] * SEL_LEN + sel_off
        t_mask = (tok <= tq[None, None, :, None, None])[:, :, None]
        s_s = jnp.einsum('bqghd,bgqnld->bghqnl', qb, gk, preferred_element_type=f32) * scale
        s_s = jnp.where(t_mask, s_s, NEG)
        sh = s_s.shape
        p_s = jax.nn.softmax(s_s.reshape(sh[:4] + (sh[4] * sh[5],)), axis=-1).reshape(sh)
        o_s = jnp.einsum('bghqnl,bgqnld->bqghd', p_s.astype(gv.dtype), gv)

        kwb = lax.dynamic_slice_in_dim(kw_pad, qs, Q_BLOCK + WINDOW, axis=1)
        vwb = lax.dynamic_slice_in_dim(vw_pad, qs, Q_BLOCK + WINDOW, axis=1)
        kpos = qs - WINDOW + win_off
        w_mask = ((kpos[None, :] <= tq[:, None]) & (kpos[None, :] > tq[:, None] - WINDOW)
                  & (kpos[None, :] >= 0))
        s_w = jnp.einsum('bqghd,bkgd->bghqk', qb, kwb, preferred_element_type=f32) * scale
        p_w = jax.nn.softmax(jnp.where(w_mask, s_w, NEG), axis=-1)
        o_w = jnp.einsum('bghqk,bkgd->bqghd', p_w.astype(vwb.dtype), vwb)

        return gb[..., 0:1] * o_c + gb[..., 1:2] * o_s + gb[..., 2:3] * o_w

    def to_blocks(t):
        return jnp.moveaxis(t.reshape((B, nqb, Q_BLOCK) + t.shape[2:]), 1, 0)

    out = lax.map(block, (to_blocks(q), to_blocks(gates), jnp.arange(nqb) * Q_BLOCK))
    return jnp.moveaxis(out, 0, 1).reshape(B, S, NSA_WIDTH)


def short_conv(xb, xc, xin, w_conv, b_conv):
    u = xc * xin
    y = lax.conv_general_dilated(u, w_conv[:, None, :].astype(u.dtype), window_strides=(1,),
                                 padding=[(CONV_K - 1, 0)],
                                 dimension_numbers=('NWC', 'WIO', 'NWC'),
                                 feature_group_count=CONV_WIDTH)
    return xb * (y + b_conv)


def token_mixing(u, positions, w_in, cmp_k_pos, cmp_k_w1, cmp_k_b1, cmp_k_w2,
                 cmp_v_pos, cmp_v_w1, cmp_v_b1, cmp_v_w2, conv_w, conv_b,
                 w_proj_nsa, w_proj_conv, w_out):
    B, S, _ = u.shape
    G = NSA_KV_HEADS
    points = []
    acc = 0
    for sz in SPLIT_SIZES[:-1]:
        acc += sz
        points.append(acc)
    z = u @ w_in
    (q, kc_raw, vc_raw, ks, vs, kw, vw, g_nsa, cb, cc, cx, g_a, g_b) = jnp.split(z, points, axis=-1)

    q = rope(q.reshape(B, S, NSA_HEADS, HEAD_DIM), positions).reshape(B, S, G, HPG, HEAD_DIM)
    kc = compress(rope(kc_raw.reshape(B, S, G, HEAD_DIM), positions), cmp_k_pos, cmp_k_w1, cmp_k_b1, cmp_k_w2)
    vc = compress(vc_raw.reshape(B, S, G, HEAD_DIM), cmp_v_pos, cmp_v_w1, cmp_v_b1, cmp_v_w2)
    ks = rope(ks.reshape(B, S, G, HEAD_DIM), positions)
    vs = vs.reshape(B, S, G, HEAD_DIM)
    kw = rope(kw.reshape(B, S, G, HEAD_DIM), positions)
    vw = vw.reshape(B, S, G, HEAD_DIM)
    gates = jax.nn.sigmoid(g_nsa.reshape(B, S, G, HPG, 3))

    o_nsa = nsa_attention(q, kc, vc, ks, vs, kw, vw, gates)
    o_conv = short_conv(cb, cc, cx, conv_w, conv_b)
    merged = (jax.nn.sigmoid(g_a) * (o_nsa @ w_proj_nsa)
              + jax.nn.sigmoid(g_b) * (o_conv @ w_proj_conv))
    return merged @ w_out


def setup_inputs(seed: int = 0) -> dict:
    key = jax.random.key(seed)
    ks = iter(jax.random.split(key, 40))
    f32 = jnp.float32

    def nrm(shape, scale):
        return jax.random.normal(next(ks), shape, f32) * scale

    def gain(shape):
        return 1.0 + 0.1 * jax.random.normal(next(ks), shape, f32)

    L = DEPTH
    x = jax.random.normal(next(ks), (BATCH, SEQ, D_MODEL), f32)
    p = jax.random.normal(next(ks), (DEPTH, BATCH, SEQ, PLE_DIM), f32)
    offsets = jax.random.randint(next(ks), (BATCH, 1), 0, 4096, dtype=jnp.int32)
    positions = offsets + jnp.arange(SEQ, dtype=jnp.int32)[None, :]
    return {
        "x": x,
        "p": p,
        "positions": positions,
        "ffn1_norm": gain((L, D_MODEL)),
        "ffn1_w_gate": nrm((L, D_MODEL, D_FF), D_MODEL ** -0.5),
        "ffn1_w_up": nrm((L, D_MODEL, D_FF), D_MODEL ** -0.5),
        "ffn1_w_down": nrm((L, D_FF, D_MODEL), D_FF ** -0.5),
        "mix_norm": gain((L, D_MODEL)),
        "w_in": nrm((L, D_MODEL, IN_COLS), D_MODEL ** -0.5),
        "cmp_k_pos": nrm((L, CMP_LEN, HEAD_DIM), 0.1),
        "cmp_k_w1": nrm((L, CMP_LEN * HEAD_DIM, CMP_HIDDEN), (CMP_LEN * HEAD_DIM) ** -0.5),
        "cmp_k_b1": nrm((L, CMP_HIDDEN), 0.01),
        "cmp_k_w2": nrm((L, CMP_HIDDEN, HEAD_DIM), CMP_HIDDEN ** -0.5),
        "cmp_v_pos": nrm((L, CMP_LEN, HEAD_DIM), 0.1),
        "cmp_v_w1": nrm((L, CMP_LEN * HEAD_DIM, CMP_HIDDEN), (CMP_LEN * HEAD_DIM) ** -0.5),
        "cmp_v_b1": nrm((L, CMP_HIDDEN), 0.01),
        "cmp_v_w2": nrm((L, CMP_HIDDEN, HEAD_DIM), CMP_HIDDEN ** -0.5),
        "conv_w": nrm((L, CONV_K, CONV_WIDTH), 0.5),
        "conv_b": nrm((L, CONV_WIDTH), 0.01),
        "w_proj_nsa": nrm((L, NSA_WIDTH, D_MODEL), NSA_WIDTH ** -0.5),
        "w_proj_conv": nrm((L, CONV_WIDTH, D_MODEL), CONV_WIDTH ** -0.5),
        "w_out": nrm((L, D_MODEL, D_MODEL), D_MODEL ** -0.5),
        "ffn2_norm": gain((L, D_MODEL)),
        "ffn2_w_gate": nrm((L, D_MODEL, D_FF), D_MODEL ** -0.5),
        "ffn2_w_up": nrm((L, D_MODEL, D_FF), D_MODEL ** -0.5),
        "ffn2_w_down": nrm((L, D_FF, D_MODEL), D_FF ** -0.5),
        "ple_norm": gain((L, D_MODEL)),
        "ple_w_gate": nrm((L, D_MODEL, D_MODEL), D_MODEL ** -0.5),
        "ple_w_proj": nrm((L, PLE_DIM, D_MODEL), PLE_DIM ** -0.5),
        "final_norm": gain((D_MODEL,)),
    }


def reference(x, p, positions, ffn1_norm, ffn1_w_gate, ffn1_w_up, ffn1_w_down,
              mix_norm, w_in, cmp_k_pos, cmp_k_w1, cmp_k_b1, cmp_k_w2,
              cmp_v_pos, cmp_v_w1, cmp_v_b1, cmp_v_w2, conv_w, conv_b,
              w_proj_nsa, w_proj_conv, w_out, ffn2_norm, ffn2_w_gate, ffn2_w_up, ffn2_w_down,
              ple_norm, ple_w_gate, ple_w_proj, final_norm):
    h = x
    for i in range(DEPTH):
        h = h + 0.5 * swiglu(rms_norm(h, ffn1_norm[i]), ffn1_w_gate[i], ffn1_w_up[i], ffn1_w_down[i])
        h = h + token_mixing(rms_norm(h, mix_norm[i]), positions, w_in[i],
                             cmp_k_pos[i], cmp_k_w1[i], cmp_k_b1[i], cmp_k_w2[i],
                             cmp_v_pos[i], cmp_v_w1[i], cmp_v_b1[i], cmp_v_w2[i],
                             conv_w[i], conv_b[i], w_proj_nsa[i], w_proj_conv[i], w_out[i])
        h = h + 0.5 * swiglu(rms_norm(h, ffn2_norm[i]), ffn2_w_gate[i], ffn2_w_up[i], ffn2_w_down[i])
        h = h + jax.nn.sigmoid(rms_norm(h, ple_norm[i]) @ ple_w_gate[i]) * (p[i] @ ple_w_proj[i])
    return rms_norm(h, final_norm)
```

```python
import functools

import jax
import jax.numpy as jnp
from jax import lax
from jax.experimental import pallas as pl
from jax.experimental.pallas import tpu as pltpu

D_MODEL = 1024
D_FF = 2816
PLE_DIM = 256
EPS = 1e-6
ROPE_THETA = 10000.0

NSA_HEADS = 8
NSA_KV_HEADS = 2
HEAD_DIM = 64
HPG = NSA_HEADS // NSA_KV_HEADS
NSA_WIDTH = NSA_HEADS * HEAD_DIM
KV_WIDTH = NSA_KV_HEADS * HEAD_DIM
CMP_LEN = 32
CMP_STRIDE = 16
CMP_HIDDEN = 256
SEL_LEN = 64
N_SEL = 16
N_LOCAL = 2
WINDOW = 512
Q_BLOCK = 128
NEG = -1e30
FORCED_SCORE = 1e4
INVALID_SCORE = -1e4
CONV_WIDTH = 512
CONV_K = 3

SEL_SHIFT = SEL_LEN.bit_length() - 1
HEAD_SHIFT = HEAD_DIM.bit_length() - 1

LANES = 128
VMEM_LIMIT = 58 * 1024 * 1024

HEAD_TM = 512
TAIL_TM = 256
FF_CHUNK = 1408
SEL_CHUNK = 512
GATE_PAD = LANES

BF16 = jnp.bfloat16
F32 = jnp.float32


def _rms(x, g):
    return x * lax.rsqrt(jnp.mean(x * x, axis=-1, keepdims=True) + EPS) * g


def _dot(a, b):
    return jnp.dot(a, b, preferred_element_type=F32)


def _dot_nt(a, b):
    return lax.dot_general(a, b, (((1,), (1,)), ((), ())), preferred_element_type=F32)


def _swiglu_half(xn, wg_ref, wu_ref, wd_ref):
    acc = jnp.zeros((xn.shape[0], D_MODEL), F32)
    for c in range(D_FF // FF_CHUNK):
        sl = slice(c * FF_CHUNK, (c + 1) * FF_CHUNK)
        gate = _dot(xn, wg_ref[:, sl])
        up = _dot(xn, wu_ref[:, sl])
        act = (gate * jax.nn.sigmoid(gate) * up).astype(BF16)
        acc = acc + _dot(act, wd_ref[sl, :])
    return acc


def _head_kernel(x_ref, pos_ref, invf_ref, n1_ref, wg_ref, wu_ref, wd_ref, n2_ref,
                 wq_ref, wkv_ref, wgn_ref,
                 h1_ref, q_ref, kc_ref, vc_ref, ks_ref, vs_ref, kw_ref, vw_ref, gate_ref):
    x = x_ref[...]
    xn = _rms(x, n1_ref[...]).astype(BF16)
    h1 = x + 0.5 * _swiglu_half(xn, wg_ref, wu_ref, wd_ref)
    h1_ref[...] = h1
    u = _rms(h1, n2_ref[...]).astype(BF16)

    tm = x.shape[0]
    ang = pos_ref[...] * invf_ref[...]
    def low_half(width):
        lane = lax.broadcasted_iota(jnp.int32, (tm, width), 1)
        return (lane & (HEAD_DIM - 1)) < (HEAD_DIM // 2)

    cos = jnp.cos(ang)
    sin = jnp.sin(ang)
    sin_signed = jnp.where(low_half(LANES), -sin, sin)

    def rope(z):
        width = z.shape[1]
        reps = width // LANES
        c = jnp.concatenate([cos] * reps, axis=1) if reps > 1 else cos
        s = jnp.concatenate([sin_signed] * reps, axis=1) if reps > 1 else sin_signed
        partner = jnp.where(low_half(width), pltpu.roll(z, width - HEAD_DIM // 2, 1),
                            pltpu.roll(z, HEAD_DIM // 2, 1))
        return z * c + partner * s

    zq = _dot(u, wq_ref[...])
    q_ref[...] = (rope(zq) * (HEAD_DIM ** -0.5)).astype(BF16)
    zkv = _dot(u, wkv_ref[...])
    kc_ref[...] = rope(zkv[:, 0 * KV_WIDTH:1 * KV_WIDTH]).astype(BF16)
    vc_ref[...] = zkv[:, 1 * KV_WIDTH:2 * KV_WIDTH].astype(BF16)
    ks_ref[...] = rope(zkv[:, 2 * KV_WIDTH:3 * KV_WIDTH]).astype(BF16)
    vs_ref[...] = zkv[:, 3 * KV_WIDTH:4 * KV_WIDTH].astype(BF16)
    kw_ref[...] = rope(zkv[:, 4 * KV_WIDTH:5 * KV_WIDTH]).astype(BF16)
    vw_ref[...] = zkv[:, 5 * KV_WIDTH:6 * KV_WIDTH].astype(BF16)
    gate_ref[...] = jax.nn.sigmoid(_dot(u, wgn_ref[...]))


def _const_spec(shape):
    nd = len(shape)
    return pl.BlockSpec(shape, lambda *_: (0,) * nd, pipeline_mode=pl.Buffered(1))


def _head_call(x2, pos, invf, n1, wg, wu, wd, n2, wq, wkv, wgn):
    t = x2.shape[0]
    tm = HEAD_TM

    def row(width):
        return pl.BlockSpec((tm, width), lambda i: (i, 0))

    kv_shape = jax.ShapeDtypeStruct((t, KV_WIDTH), BF16)
    return pl.pallas_call(
        _head_kernel,
        grid=(t // tm,),
        in_specs=[row(D_MODEL), row(1), _const_spec((1, LANES)), _const_spec((1, D_MODEL)),
                  _const_spec((D_MODEL, D_FF)), _const_spec((D_MODEL, D_FF)), _const_spec((D_FF, D_MODEL)),
                  _const_spec((1, D_MODEL)), _const_spec((D_MODEL, NSA_WIDTH)),
                  _const_spec((D_MODEL, 6 * KV_WIDTH)), _const_spec((D_MODEL, NSA_KV_HEADS * GATE_PAD))],
        out_specs=[row(D_MODEL), row(NSA_WIDTH)] + [row(KV_WIDTH)] * 6 + [row(NSA_KV_HEADS * GATE_PAD)],
        out_shape=[jax.ShapeDtypeStruct((t, D_MODEL), F32), jax.ShapeDtypeStruct((t, NSA_WIDTH), BF16)]
                  + [kv_shape] * 6 + [jax.ShapeDtypeStruct((t, NSA_KV_HEADS * GATE_PAD), F32)],
        compiler_params=pltpu.CompilerParams(dimension_semantics=("arbitrary",),
                                             vmem_limit_bytes=VMEM_LIMIT),
        name="head",
    )(x2, pos, invf, n1, wg, wu, wd, n2, wq, wkv, wgn)


def _compress_kernel(kx_ref, vx_ref,
                     kpa_ref, kpb_ref, kwa_ref, kwb_ref, kb1_ref, kw2_ref,
                     vpa_ref, vpb_ref, vwa_ref, vwb_ref, vb1_ref, vw2_ref,
                     kc_ref, vc_ref):
    def one(x_ref, pa_ref, pb_ref, wa_ref, wb_ref, b1_ref, w2_ref, o_ref):
        x = x_ref[0].astype(F32)
        first = _dot((x + pa_ref[...]).astype(BF16), wa_ref[...])
        second = _dot((x + pb_ref[...]).astype(BF16), wb_ref[...])
        n = first.shape[0]
        hidden = first + pltpu.roll(second, n - 1, 0) + b1_ref[...]
        o_ref[0] = _dot(jax.nn.gelu(hidden).astype(BF16), w2_ref[...]).astype(BF16)

    one(kx_ref, kpa_ref, kpb_ref, kwa_ref, kwb_ref, kb1_ref, kw2_ref, kc_ref)
    one(vx_ref, vpa_ref, vpb_ref, vwa_ref, vwb_ref, vb1_ref, vw2_ref, vc_ref)


def _compress_call(kx, vx, kparams, vparams):
    b, nrow, width = kx.shape
    hid = NSA_KV_HEADS * CMP_HIDDEN
    x_spec = pl.BlockSpec((1, nrow, width), lambda i: (i, 0, 0))
    w_specs = [_const_spec((1, width)), _const_spec((1, width)), _const_spec((width, hid)),
               _const_spec((width, hid)), _const_spec((1, hid)), _const_spec((hid, KV_WIDTH))]
    o_spec = pl.BlockSpec((1, nrow, KV_WIDTH), lambda i: (i, 0, 0))
    o_shape = jax.ShapeDtypeStruct((b, nrow, KV_WIDTH), BF16)
    return pl.pallas_call(
        _compress_kernel,
        grid=(b,),
        in_specs=[x_spec, x_spec] + w_specs + w_specs,
        out_specs=[o_spec, o_spec],
        out_shape=[o_shape, o_shape],
        compiler_params=pltpu.CompilerParams(dimension_semantics=("arbitrary",),
                                             vmem_limit_bytes=VMEM_LIMIT),
        name="compress",
    )(kx, vx, *kparams, *vparams)


def _softmax_rows(s):
    m = jnp.max(s, axis=-1, keepdims=True)
    e = jnp.exp(s - m)
    return e / jnp.sum(e, axis=-1, keepdims=True)


def _attn_kernel(q_ref, gate_ref, kc_ref, vc_ref, ks_ref, vs_ref, kw_ref, vw_ref, ovl_ref, o_ref):
    qb = Q_BLOCK
    i = pl.program_id(1)
    g = pl.program_id(2)
    qs = i * qb
    n_cmp = kc_ref.shape[1]
    n_blk = ovl_ref.shape[1]

    lane = lax.broadcasted_iota(jnp.int32, (qb, LANES), 1)
    in_group = (lane >> HEAD_SHIFT) == g
    tq = qs + lax.broadcasted_iota(jnp.int32, (qb, 1), 0)

    q_all = q_ref[0].astype(F32)
    heads = []
    for hh in range(HPG):
        pair = q_all[:, (hh // 2) * LANES:(hh // 2 + 1) * LANES]
        moved = jnp.where(g == hh % 2, pair, pltpu.roll(pair, HEAD_DIM, 1))
        heads.append(jnp.where(in_group, moved, 0.0))
    qg = jnp.concatenate(heads, axis=0).astype(BF16)

    def mask_heads(mask, s):
        s3 = s.reshape(HPG, qb, s.shape[-1])
        return jnp.where(mask[None], s3, NEG)

    c_last = lax.broadcasted_iota(jnp.int32, (qb, n_cmp), 1) * CMP_STRIDE + (CMP_LEN - 1)
    c_mask = c_last <= tq
    p_c = _softmax_rows(mask_heads(c_mask, _dot_nt(qg, kc_ref[0])))
    p_c = p_c * (tq >= CMP_LEN - 1).astype(F32)[None]
    o_c = _dot(p_c.reshape(HPG * qb, n_cmp).astype(BF16), vc_ref[0])

    p_sum = jnp.sum(p_c, axis=0)
    ovl = ovl_ref[...]
    p_hi = p_sum.astype(BF16)
    r1 = p_sum - p_hi.astype(F32)
    p_mid = r1.astype(BF16)
    p_lo = (r1 - p_mid.astype(F32)).astype(BF16)
    imp = _dot(p_hi, ovl) + _dot(p_mid, ovl) + _dot(p_lo, ovl)
    s_idx = lax.broadcasted_iota(jnp.int32, (qb, n_blk), 1)
    blk_q = tq >> SEL_SHIFT
    valid = s_idx <= blk_q
    forced = (s_idx == 0) | (valid & (s_idx > blk_q - N_LOCAL))
    score = jnp.where(forced, FORCED_SCORE, jnp.where(valid, imp, INVALID_SCORE))
    member = jnp.zeros((qb, n_blk), F32)
    s_idx_f = s_idx.astype(F32)
    for _ in range(min(N_SEL, n_blk)):
        top = jnp.max(score, axis=-1, keepdims=True)
        first = jnp.min(jnp.where(score == top, s_idx_f, float(n_blk)), axis=-1, keepdims=True)
        hit = s_idx_f == first
        member = jnp.where(hit, 1.0, member)
        score = jnp.where(hit, -jnp.inf, score)
    member = member.astype(BF16)

    ck = SEL_CHUNK
    blk_row = lax.broadcasted_iota(jnp.int32, (n_blk, ck), 0)
    key_col = lax.broadcasted_iota(jnp.int32, (n_blk, ck), 1)
    kpos_col = lax.broadcasted_iota(jnp.int32, (qb, ck), 1)

    def sel_step(j, carry):
        m_prev, l_prev, acc = carry
        start = pl.multiple_of(j * ck, ck)
        expand = (((key_col + start) >> SEL_SHIFT) == blk_row).astype(BF16)
        chosen = _dot(member, expand) > 0.5
        mask = chosen & (kpos_col + start <= tq)
        s = mask_heads(mask, _dot_nt(qg, ks_ref[0, pl.ds(start, ck), :]))
        m_new = jnp.maximum(m_prev, jnp.max(s, axis=-1, keepdims=True))
        alpha = jnp.exp(m_prev - m_new)
        p = jnp.exp(s - m_new)
        l_new = alpha * l_prev + jnp.sum(p, axis=-1, keepdims=True)
        pv = _dot(p.reshape(HPG * qb, ck).astype(BF16), vs_ref[0, pl.ds(start, ck), :])
        acc = alpha * acc + pv.reshape(HPG, qb, LANES)
        return m_new, l_new, acc

    n_chunks = (qs + qb - 1) // ck + 1
    init = (jnp.full((HPG, qb, 1), NEG, F32), jnp.zeros((HPG, qb, 1), F32),
            jnp.zeros((HPG, qb, LANES), F32))
    _, l_s, acc_s = lax.fori_loop(0, n_chunks, sel_step, init)
    o_s = acc_s / l_s

    span = WINDOW + qb
    w_start = pl.multiple_of(jnp.maximum(qs - WINDOW, 0), qb)
    kpos = w_start + lax.broadcasted_iota(jnp.int32, (qb, span), 1)
    w_mask = (kpos <= tq) & (kpos > tq - WINDOW)
    p_w = _softmax_rows(mask_heads(w_mask, _dot_nt(qg, kw_ref[0, pl.ds(w_start, span), :])))
    o_w = _dot(p_w.reshape(HPG * qb, span).astype(BF16), vw_ref[0, pl.ds(w_start, span), :])

    o_c = o_c.reshape(HPG, qb, LANES)
    o_w = o_w.reshape(HPG, qb, LANES)
    gates = gate_ref[0]
    mixed = []
    for hh in range(HPG):
        g_c = gates[:, 3 * hh + 0:3 * hh + 1]
        g_s = gates[:, 3 * hh + 1:3 * hh + 2]
        g_w = gates[:, 3 * hh + 2:3 * hh + 3]
        mixed.append(g_c * o_c[hh] + g_s * o_s[hh] + g_w * o_w[hh])
    for pr in range(HPG // 2):
        even = jnp.where(g == 0, mixed[2 * pr], pltpu.roll(mixed[2 * pr], HEAD_DIM, 1))
        odd = jnp.where(g == 1, mixed[2 * pr + 1], pltpu.roll(mixed[2 * pr + 1], HEAD_DIM, 1))
        o_ref[0, :, pr * LANES:(pr + 1) * LANES] = jnp.where(lane < HEAD_DIM, even, odd).astype(BF16)


def _attn_call(q, gates, kc, vc, ks, vs, kw, vw, overlap):
    b, s, _ = q.shape
    n_cmp = kc.shape[1]
    gw = NSA_WIDTH // NSA_KV_HEADS
    seq_spec = pl.BlockSpec((1, s, KV_WIDTH), lambda bi, i, g: (bi, 0, 0))
    cmp_spec = pl.BlockSpec((1, n_cmp, KV_WIDTH), lambda bi, i, g: (bi, 0, 0))
    return pl.pallas_call(
        _attn_kernel,
        grid=(b, s // Q_BLOCK, NSA_KV_HEADS),
        in_specs=[pl.BlockSpec((1, Q_BLOCK, gw), lambda bi, i, g: (bi, i, g)),
                  pl.BlockSpec((1, Q_BLOCK, GATE_PAD), lambda bi, i, g: (bi, i, g)),
                  cmp_spec, cmp_spec, seq_spec, seq_spec, seq_spec, seq_spec,
                  _const_spec(overlap.shape)],
        out_specs=pl.BlockSpec((1, Q_BLOCK, gw), lambda bi, i, g: (bi, i, g)),
        out_shape=jax.ShapeDtypeStruct((b, s, NSA_WIDTH), BF16),
        compiler_params=pltpu.CompilerParams(dimension_semantics=("arbitrary",) * 3,
                                             vmem_limit_bytes=VMEM_LIMIT),
        name="attention",
    )(q, gates, kc, vc, ks, vs, kw, vw, overlap)


def _tail_kernel(h1_ref, on_ref, p_ref, nm_ref, wc_ref, wga_ref, wgb_ref, cw_ref, cb_ref,
                 wa_ref, wb_ref, wo_ref, n2_ref, wg_ref, wu_ref, wd_ref,
                 np_ref, pg_ref, pp_ref, nf_ref, out_ref, carry_ref, *, tiles_per_seq):
    i = pl.program_id(0)
    h1 = h1_ref[...]
    tm = h1.shape[0]
    u = _rms(h1, nm_ref[...]).astype(BF16)

    zc = _dot(u, wc_ref[...])
    gate_b = zc[:, :CONV_WIDTH]
    v = zc[:, CONV_WIDTH:2 * CONV_WIDTH] * zc[:, 2 * CONV_WIDTH:]

    @pl.when(i % tiles_per_seq == 0)
    def _():
        carry_ref[...] = jnp.zeros_like(carry_ref)

    prev = carry_ref[...]
    row = lax.broadcasted_iota(jnp.int32, (tm, CONV_WIDTH), 0)
    v1 = jnp.where(row == 0, prev[7:8], pltpu.roll(v, 1, 0))
    v2 = jnp.where(row == 0, prev[6:7], jnp.where(row == 1, prev[7:8], pltpu.roll(v, 2, 0)))
    carry_ref[...] = v[tm - 8:]
    cw = cw_ref[...]
    y = cw[0:1] * v2 + cw[1:2] * v1 + cw[2:3] * v
    o_conv = (gate_b * (y + cb_ref[...])).astype(BF16)

    merged = (jax.nn.sigmoid(_dot(u, wga_ref[...])) * _dot(on_ref[...], wa_ref[...])
              + jax.nn.sigmoid(_dot(u, wgb_ref[...])) * _dot(o_conv, wb_ref[...]))
    h2 = h1 + _dot(merged.astype(BF16), wo_ref[...])

    h3 = h2 + 0.5 * _swiglu_half(_rms(h2, n2_ref[...]).astype(BF16), wg_ref, wu_ref, wd_ref)

    gate_p = jax.nn.sigmoid(_dot(_rms(h3, np_ref[...]).astype(BF16), pg_ref[...]))
    h4 = h3 + gate_p * _dot(p_ref[...].astype(BF16), pp_ref[...])
    out_ref[...] = _rms(h4, nf_ref[...])


def _tail_call(h1, o_nsa, p2, seq, nm, wc, wga, wgb, cw, cb, wa, wb, wo, n2, wg, wu, wd, npl, pg, pp, nf):
    t = h1.shape[0]
    tm = TAIL_TM

    def row(width):
        return pl.BlockSpec((tm, width), lambda i: (i, 0))

    consts = [nm, wc, wga, wgb, cw, cb, wa, wb, wo, n2, wg, wu, wd, npl, pg, pp, nf]
    return pl.pallas_call(
        functools.partial(_tail_kernel, tiles_per_seq=seq // tm),
        grid=(t // tm,),
        in_specs=[row(D_MODEL), row(NSA_WIDTH), row(PLE_DIM)] + [_const_spec(c.shape) for c in consts],
        out_specs=row(D_MODEL),
        out_shape=jax.ShapeDtypeStruct((t, D_MODEL), F32),
        scratch_shapes=[pltpu.VMEM((8, CONV_WIDTH), F32)],
        compiler_params=pltpu.CompilerParams(dimension_semantics=("arbitrary",),
                                             vmem_limit_bytes=VMEM_LIMIT),
        name="tail",
    )(h1, o_nsa, p2, *consts)


def _compress_params(pos, w1, b1, w2):
    eye = jnp.eye(NSA_KV_HEADS, dtype=F32)
    half = CMP_LEN // 2
    w1r = w1.reshape(CMP_LEN, HEAD_DIM, CMP_HIDDEN)

    def expand(w):
        return jnp.einsum('ldn,gh->lgdhn', w, eye).reshape(half * KV_WIDTH, NSA_KV_HEADS * CMP_HIDDEN).astype(BF16)

    def pos_row(pp):
        return jnp.broadcast_to(pp[:, None, :], (half, NSA_KV_HEADS, HEAD_DIM)).reshape(1, half * KV_WIDTH)

    w2b = jnp.einsum('nd,gh->gnhd', w2, eye).reshape(NSA_KV_HEADS * CMP_HIDDEN, KV_WIDTH).astype(BF16)
    b1t = jnp.tile(b1, NSA_KV_HEADS).reshape(1, NSA_KV_HEADS * CMP_HIDDEN)
    return (pos_row(pos[:half]), pos_row(pos[half:]), expand(w1r[:half]), expand(w1r[half:]), b1t, w2b)


def kernel(x, p, positions, ffn1_norm, ffn1_w_gate, ffn1_w_up, ffn1_w_down, mix_norm, w_in, cmp_k_pos, cmp_k_w1, cmp_k_b1, cmp_k_w2, cmp_v_pos, cmp_v_w1, cmp_v_b1, cmp_v_w2, conv_w, conv_b, w_proj_nsa, w_proj_conv, w_out, ffn2_norm, ffn2_w_gate, ffn2_w_up, ffn2_w_down, ple_norm, ple_w_gate, ple_w_proj, final_norm):
    b, s, d = x.shape
    t = b * s
    depth = ffn1_norm.shape[0]
    assert depth == 1 and d == D_MODEL and s % SEL_CHUNK == 0 and s % HEAD_TM == 0

    half = HEAD_DIM // 2
    inv_freq = ROPE_THETA ** (-jnp.arange(half, dtype=F32) / half)
    invf = jnp.tile(inv_freq, LANES // half).reshape(1, LANES)
    pos = positions.astype(F32).reshape(t, 1)

    n_cmp_rows = s // CMP_STRIDE
    n_blk = s // SEL_LEN
    c_start = jnp.arange(n_cmp_rows) * CMP_STRIDE
    s_start = jnp.arange(n_blk) * SEL_LEN
    overlap = (jnp.clip(jnp.minimum(c_start[:, None] + CMP_LEN, s_start[None, :] + SEL_LEN)
                        - jnp.maximum(c_start[:, None], s_start[None, :]), 0, None).astype(F32)
               / CMP_LEN).astype(BF16)

    row = lambda v: v.reshape(1, -1)
    h = x.reshape(t, d)
    for li in range(depth):
        wi = w_in[li]
        o = 0
        parts = []
        for sz in (NSA_WIDTH, 6 * KV_WIDTH, 3 * NSA_HEADS, 3 * CONV_WIDTH, D_MODEL, D_MODEL):
            parts.append(wi[:, o:o + sz])
            o += sz
        wq, wkv, wgn, wc, wga, wgb = parts
        per_group = 3 * HPG
        wgn = jnp.pad(wgn.reshape(d, NSA_KV_HEADS, per_group),
                      ((0, 0), (0, 0), (0, GATE_PAD - per_group))).reshape(d, NSA_KV_HEADS * GATE_PAD)

        h1, q, kc_r, vc_r, ks, vs, kw, vw, gates = _head_call(
            h, pos, invf, row(ffn1_norm[li]), ffn1_w_gate[li].astype(BF16), ffn1_w_up[li].astype(BF16),
            ffn1_w_down[li].astype(BF16), row(mix_norm[li]), wq.astype(BF16), wkv.astype(BF16),
            wgn.astype(BF16))

        chunk = CMP_STRIDE * KV_WIDTH
        kc, vc = _compress_call(
            kc_r.reshape(b, n_cmp_rows, chunk), vc_r.reshape(b, n_cmp_rows, chunk),
            _compress_params(cmp_k_pos[li], cmp_k_w1[li], cmp_k_b1[li], cmp_k_w2[li]),
            _compress_params(cmp_v_pos[li], cmp_v_w1[li], cmp_v_b1[li], cmp_v_w2[li]))

        seq3 = lambda a: a.reshape(b, s, a.shape[-1])
        o_nsa = _attn_call(seq3(q), seq3(gates), kc, vc, seq3(ks), seq3(vs), seq3(kw), seq3(vw), overlap)

        h = _tail_call(
            h1, o_nsa.reshape(t, NSA_WIDTH), p[li].reshape(t, PLE_DIM), s,
            row(mix_norm[li]), wc.astype(BF16), wga.astype(BF16), wgb.astype(BF16),
            conv_w[li], row(conv_b[li]), w_proj_nsa[li].astype(BF16), w_proj_conv[li].astype(BF16),
            w_out[li].astype(BF16), row(ffn2_norm[li]), ffn2_w_gate[li].astype(BF16),
            ffn2_w_up[li].astype(BF16), ffn2_w_down[li].astype(BF16), row(ple_norm[li]),
            ple_w_gate[li].astype(BF16), ple_w_proj[li].astype(BF16), row(final_norm))
    return h.reshape(b, s, d)
```

```python
import functools
import math

import jax
import jax.numpy as jnp
from jax import lax
from jax.experimental import pallas as pl
from jax.experimental.pallas import tpu as pltpu

D_MODEL = 1024
D_FF = 2816
PLE_DIM = 256
EPS = 1e-6
ROPE_THETA = 10000.0

NSA_HEADS = 8
NSA_KV_HEADS = 2
HEAD_DIM = 64
HPG = NSA_HEADS // NSA_KV_HEADS
NSA_WIDTH = NSA_HEADS * HEAD_DIM
KV_WIDTH = NSA_KV_HEADS * HEAD_DIM
CMP_LEN = 32
CMP_STRIDE = 16
CMP_HIDDEN = 256
SEL_LEN = 64
N_SEL = 16
N_LOCAL = 2
WINDOW = 512
Q_BLOCK = 128
FORCED_SCORE = 1e4
INVALID_SCORE = -1e4
CONV_WIDTH = 512
CONV_K = 3

SEL_SHIFT = SEL_LEN.bit_length() - 1

LANES = 128
SUBLANES = 8
VMEM_LIMIT = 58 * 1024 * 1024

HEAD_TM = 512
TAIL_TM = 256
FF_CHUNK = 1408
SEL_CHUNK = 512
GATE_PAD = LANES
BIAS_LANES = LANES - HEAD_DIM
NEG = -(2.0 ** 100)
Q_SCALE = HEAD_DIM ** -0.5 * math.log2(math.e)

BF16 = jnp.bfloat16
F32 = jnp.float32


def _rms(x, g):
    return x * lax.rsqrt(jnp.mean(x * x, axis=-1, keepdims=True) + EPS) * g


def _dot(a, b):
    return jnp.dot(a, b, preferred_element_type=F32)


def _dot_nt(a, b):
    return lax.dot_general(a, b, (((1,), (1,)), ((), ())), preferred_element_type=F32)


def _swiglu_half(xn, wg_ref, wu_ref, wd_ref):
    acc = jnp.zeros((xn.shape[0], D_MODEL), F32)
    for c in range(D_FF // FF_CHUNK):
        sl = slice(c * FF_CHUNK, (c + 1) * FF_CHUNK)
        gate = _dot(xn, wg_ref[:, sl])
        up = _dot(xn, wu_ref[:, sl])
        act = (gate * jax.nn.sigmoid(gate) * up).astype(BF16)
        acc = acc + _dot(act, wd_ref[sl, :])
    return acc


def _head_kernel(x_ref, pos_ref, invf_ref, n1_ref, wg_ref, wu_ref, wd_ref, n2_ref,
                 wq_ref, wkv_ref, wgn_ref,
                 h1_ref, q_ref, kc_ref, vc_ref, ks_ref, vs_ref, kw_ref, vw_ref, gate_ref, *, seq):
    x = x_ref[...]
    xn = _rms(x, n1_ref[...]).astype(BF16)
    h1 = x + 0.5 * _swiglu_half(xn, wg_ref, wu_ref, wd_ref)
    h1_ref[...] = h1
    u = _rms(h1, n2_ref[...]).astype(BF16)

    tm = x.shape[0]
    ang = pos_ref[...] * invf_ref[...]

    def low_half(width):
        lane = lax.broadcasted_iota(jnp.int32, (tm, width), 1)
        return (lane & (HEAD_DIM - 1)) < (HEAD_DIM // 2)

    cos = jnp.cos(ang)
    sin = jnp.sin(ang)
    sin_signed = jnp.where(low_half(LANES), -sin, sin)

    def rope(z):
        width = z.shape[1]
        reps = width // LANES
        c = jnp.concatenate([cos] * reps, axis=1) if reps > 1 else cos
        s = jnp.concatenate([sin_signed] * reps, axis=1) if reps > 1 else sin_signed
        partner = jnp.where(low_half(width), pltpu.roll(z, width - HEAD_DIM // 2, 1),
                            pltpu.roll(z, HEAD_DIM // 2, 1))
        return z * c + partner * s

    zq = _dot(u, wq_ref[...])
    q_ref[...] = (rope(zq) * Q_SCALE).astype(BF16)

    lane = lax.broadcasted_iota(jnp.int32, (tm, LANES), 1)
    is_dim = lane < HEAD_DIM
    ones_col = jnp.where(lane == HEAD_DIM, 1.0, 0.0)
    seq_pos = (pl.program_id(0) * tm + lax.broadcasted_iota(jnp.int32, (tm, LANES), 0)) & (seq - 1)
    blk_onehot = jnp.where(lane - HEAD_DIM == ((seq_pos >> SEL_SHIFT) & (BIAS_LANES - 1)), 1.0, 0.0)

    def per_group(z, extra):
        g0 = jnp.where(is_dim, z, extra)
        g1 = jnp.where(is_dim, pltpu.roll(z, HEAD_DIM, 1), extra)
        return jnp.concatenate([g0, g1], axis=1).astype(BF16)

    zkv = _dot(u, wkv_ref[...])
    kc_ref[...] = rope(zkv[:, 0 * KV_WIDTH:1 * KV_WIDTH]).astype(BF16)
    vc_ref[...] = zkv[:, 1 * KV_WIDTH:2 * KV_WIDTH].astype(BF16)
    ks_ref[...] = per_group(rope(zkv[:, 2 * KV_WIDTH:3 * KV_WIDTH]), blk_onehot)
    vs_ref[...] = per_group(zkv[:, 3 * KV_WIDTH:4 * KV_WIDTH], ones_col)
    kw_ref[...] = per_group(rope(zkv[:, 4 * KV_WIDTH:5 * KV_WIDTH]), 0.0)
    vw_ref[...] = per_group(zkv[:, 5 * KV_WIDTH:6 * KV_WIDTH], ones_col)
    gate_ref[...] = jax.nn.sigmoid(_dot(u, wgn_ref[...]))


def _const_spec(shape):
    nd = len(shape)
    return pl.BlockSpec(shape, lambda *_: (0,) * nd, pipeline_mode=pl.Buffered(1))


def _head_call(x2, pos, invf, n1, wg, wu, wd, n2, wq, wkv, wgn, seq):
    t = x2.shape[0]
    tm = HEAD_TM
    assert seq % tm == 0 and seq & (seq - 1) == 0

    def row(width):
        return pl.BlockSpec((tm, width), lambda i: (i, 0))

    def out(width, dtype):
        return jax.ShapeDtypeStruct((t, width), dtype)

    wide = NSA_KV_HEADS * LANES
    return pl.pallas_call(
        functools.partial(_head_kernel, seq=seq),
        grid=(t // tm,),
        in_specs=[row(D_MODEL), row(1), _const_spec((1, LANES)), _const_spec((1, D_MODEL)),
                  _const_spec((D_MODEL, D_FF)), _const_spec((D_MODEL, D_FF)), _const_spec((D_FF, D_MODEL)),
                  _const_spec((1, D_MODEL)), _const_spec((D_MODEL, NSA_WIDTH)),
                  _const_spec((D_MODEL, 6 * KV_WIDTH)), _const_spec((D_MODEL, NSA_KV_HEADS * GATE_PAD))],
        out_specs=[row(D_MODEL), row(NSA_WIDTH), row(KV_WIDTH), row(KV_WIDTH)] + [row(wide)] * 4
                  + [row(NSA_KV_HEADS * GATE_PAD)],
        out_shape=[out(D_MODEL, F32), out(NSA_WIDTH, BF16), out(KV_WIDTH, BF16), out(KV_WIDTH, BF16)]
                  + [out(wide, BF16)] * 4 + [out(NSA_KV_HEADS * GATE_PAD, F32)],
        compiler_params=pltpu.CompilerParams(dimension_semantics=("arbitrary",),
                                             vmem_limit_bytes=VMEM_LIMIT),
        name="head",
    )(x2, pos, invf, n1, wg, wu, wd, n2, wq, wkv, wgn)


def _compress_kernel(kx_ref, vx_ref,
                     kpa_ref, kpb_ref, kwa_ref, kwb_ref, kb1_ref, kw2_ref,
                     vpa_ref, vpb_ref, vwa_ref, vwb_ref, vb1_ref, vw2_ref,
                     kc_ref, vc_ref):
    def one(x_ref, pa_ref, pb_ref, wa_ref, wb_ref, b1_ref, w2_ref):
        x = x_ref[0].astype(F32)
        first = _dot((x + pa_ref[...]).astype(BF16), wa_ref[...])
        second = _dot((x + pb_ref[...]).astype(BF16), wb_ref[...])
        n = first.shape[0]
        hidden = first + pltpu.roll(second, n - 1, 0) + b1_ref[...]
        return _dot(jax.nn.gelu(hidden).astype(BF16), w2_ref[...])

    kc_ref[0] = one(kx_ref, kpa_ref, kpb_ref, kwa_ref, kwb_ref, kb1_ref, kw2_ref).astype(BF16)
    vc_ref[0] = one(vx_ref, vpa_ref, vpb_ref, vwa_ref, vwb_ref, vb1_ref, vw2_ref).astype(BF16)


def _compress_call(kx, vx, kparams, vparams):
    b, nrow, width = kx.shape
    hid = NSA_KV_HEADS * CMP_HIDDEN
    wide = NSA_KV_HEADS * LANES
    x_spec = pl.BlockSpec((1, nrow, width), lambda i: (i, 0, 0))
    w_specs = [_const_spec((1, width)), _const_spec((1, width)), _const_spec((width, hid)),
               _const_spec((width, hid)), _const_spec((1, hid)), _const_spec((hid, wide))]
    o_spec = pl.BlockSpec((1, nrow, wide), lambda i: (i, 0, 0))
    o_shape = jax.ShapeDtypeStruct((b, nrow, wide), BF16)
    return pl.pallas_call(
        _compress_kernel,
        grid=(b,),
        in_specs=[x_spec, x_spec] + w_specs + w_specs,
        out_specs=[o_spec, o_spec],
        out_shape=[o_shape, o_shape],
        compiler_params=pltpu.CompilerParams(dimension_semantics=("arbitrary",),
                                             vmem_limit_bytes=VMEM_LIMIT),
        name="compress",
    )(kx, vx, *kparams, *vparams)


def _attn_kernel(q_ref, gate_ref, kc_ref, vc_ref, ks_ref, vs_ref, kw_ref, vw_ref, ovl_ref, o_ref, qa_ref):
    qb = Q_BLOCK
    rows = HPG * qb
    qs = pl.program_id(2) * qb
    n_cmp = kc_ref.shape[1]
    n_blk = ovl_ref.shape[0]

    lane = lax.broadcasted_iota(jnp.int32, (qb, LANES), 1)
    is_dim = lane < HEAD_DIM
    tq = qs + lax.broadcasted_iota(jnp.int32, (qb, 1), 0)
    tq_row = qs + lax.broadcasted_iota(jnp.int32, (1, qb), 1)

    q_all = q_ref[0].astype(F32)
    q_heads = []
    for hh in range(HPG):
        pair = q_all[:, (hh // 2) * LANES:(hh // 2 + 1) * LANES]
        q_heads.append(pair if hh % 2 == 0 else pltpu.roll(pair, HEAD_DIM, 1))
    qg = jnp.concatenate([jnp.where(is_dim, qh, 0.0) for qh in q_heads], axis=0).astype(BF16)

    def mask_heads(mask, s):
        s3 = s.reshape(HPG, qb, s.shape[-1])
        return jnp.where(mask[None], s3, NEG).reshape(rows, s.shape[-1])

    def softmax_pv(s, v):
        m = jnp.max(s, axis=-1, keepdims=True)
        return _dot(jnp.exp2(s - m).astype(BF16), v)

    def normalise(acc):
        return acc / acc[:, HEAD_DIM:HEAD_DIM + 1]

    c_last = lax.broadcasted_iota(jnp.int32, (qb, n_cmp), 1) * CMP_STRIDE + (CMP_LEN - 1)
    s_c = mask_heads(c_last <= tq, _dot_nt(qg, kc_ref[0]))
    e_c = jnp.exp2(s_c - jnp.max(s_c, axis=-1, keepdims=True))
    p_c = e_c / jnp.sum(e_c, axis=-1, keepdims=True)
    p_c = (p_c.reshape(HPG, qb, n_cmp) * (tq >= CMP_LEN - 1).astype(F32)[None])
    o_c = _dot(p_c.reshape(rows, n_cmp).astype(BF16), vc_ref[0])

    p_sum = jnp.sum(p_c, axis=0)
    p_hi = p_sum.astype(BF16)
    p_lo = (p_sum - p_hi.astype(F32)).astype(BF16)
    ovl = ovl_ref[...]
    imp = _dot_nt(ovl, p_hi) + _dot_nt(ovl, p_lo)
    s_idx = lax.broadcasted_iota(jnp.int32, (n_blk, qb), 0)
    blk_q = tq_row >> SEL_SHIFT
    valid = s_idx <= blk_q
    forced = (s_idx == 0) | (valid & (s_idx > blk_q - N_LOCAL))
    score = jnp.where(forced, FORCED_SCORE, jnp.where(valid, imp, INVALID_SCORE))
    chosen = jnp.zeros((n_blk, qb), F32)
    s_idx_f = s_idx.astype(F32)
    for _ in range(min(N_SEL, n_blk)):
        top = jnp.max(score, axis=0, keepdims=True)
        first = jnp.min(jnp.where(score == top, s_idx_f, float(n_blk)), axis=0, keepdims=True)
        hit = s_idx_f == first
        chosen = jnp.where(hit, 1.0, chosen)
        score = jnp.where(hit, -jnp.inf, score)
    bias = jnp.where((chosen > 0.5) & valid, 0.0, NEG).T

    for w in range(n_blk // BIAS_LANES):
        b_w = bias[:, w * BIAS_LANES:(w + 1) * BIAS_LANES]
        b_w = jnp.concatenate([b_w, b_w], axis=1)
        qa_ref[w] = jnp.concatenate([jnp.where(is_dim, qh, b_w) for qh in q_heads], axis=0).astype(BF16)

    ck = SEL_CHUNK
    blocks_per_chunk = ck // SEL_LEN

    def sel_scores(j):
        start = pl.multiple_of(j * ck, ck)
        qa = qa_ref[j // (BIAS_LANES // blocks_per_chunk)]
        return _dot_nt(qa, ks_ref[0, pl.ds(start, ck), :]), start

    def sel_update(s, start, carry):
        m_prev, acc = carry
        m_new = jnp.maximum(m_prev, jnp.max(s, axis=-1, keepdims=True))
        pv = _dot(jnp.exp2(s - m_new).astype(BF16), vs_ref[0, pl.ds(start, ck), :])
        return m_new, jnp.exp2(m_prev - m_new) * acc + pv

    def sel_step(j, carry):
        s, start = sel_scores(j)
        return sel_update(s, start, carry)

    last = (qs + qb - 1) // ck
    carry = lax.fori_loop(0, last, sel_step, (jnp.full((rows, 1), NEG, F32), jnp.zeros((rows, LANES), F32)))
    s, start = sel_scores(last)
    kpos = start + lax.broadcasted_iota(jnp.int32, (qb, ck), 1)
    _, acc_s = sel_update(mask_heads(kpos <= tq, s), start, carry)
    o_s = normalise(acc_s)

    span = WINDOW + qb
    w_start = pl.multiple_of(jnp.maximum(qs - WINDOW, 0), qb)
    kpos = w_start + lax.broadcasted_iota(jnp.int32, (qb, span), 1)
    s_w = mask_heads((kpos <= tq) & (kpos > tq - WINDOW), _dot_nt(qg, kw_ref[0, pl.ds(w_start, span), :]))
    o_w = normalise(softmax_pv(s_w, vw_ref[0, pl.ds(w_start, span), :]))

    gates = gate_ref[0]
    mixed = []
    for hh in range(HPG):
        sl = slice(hh * qb, (hh + 1) * qb)
        g_c = gates[:, 3 * hh + 0:3 * hh + 1]
        g_s = gates[:, 3 * hh + 1:3 * hh + 2]
        g_w = gates[:, 3 * hh + 2:3 * hh + 3]
        mixed.append(g_c * o_c[sl] + g_s * o_s[sl] + g_w * o_w[sl])
    for pr in range(HPG // 2):
        odd = pltpu.roll(mixed[2 * pr + 1], HEAD_DIM, 1)
        o_ref[0, :, pr * LANES:(pr + 1) * LANES] = jnp.where(is_dim, mixed[2 * pr], odd).astype(BF16)


def _attn_call(q, gates, kc, vc, ks, vs, kw, vw, overlap_t):
    b, s, _ = q.shape
    n_cmp = kc.shape[1]
    n_blk = s // SEL_LEN
    assert n_blk % BIAS_LANES == 0 and BIAS_LANES % (SEL_CHUNK // SEL_LEN) == 0
    gw = NSA_WIDTH // NSA_KV_HEADS
    seq_spec = pl.BlockSpec((1, s, LANES), lambda bi, g, i: (bi, 0, g))
    cmp_spec = pl.BlockSpec((1, n_cmp, LANES), lambda bi, g, i: (bi, 0, g))
    return pl.pallas_call(
        _attn_kernel,
        grid=(b, NSA_KV_HEADS, s // Q_BLOCK),
        in_specs=[pl.BlockSpec((1, Q_BLOCK, gw), lambda bi, g, i: (bi, i, g)),
                  pl.BlockSpec((1, Q_BLOCK, GATE_PAD), lambda bi, g, i: (bi, i, g)),
                  cmp_spec, cmp_spec, seq_spec, seq_spec, seq_spec, seq_spec,
                  _const_spec(overlap_t.shape)],
        out_specs=pl.BlockSpec((1, Q_BLOCK, gw), lambda bi, g, i: (bi, i, g)),
        out_shape=jax.ShapeDtypeStruct((b, s, NSA_WIDTH), BF16),
        scratch_shapes=[pltpu.VMEM((n_blk // BIAS_LANES, HPG * Q_BLOCK, LANES), BF16)],
        compiler_params=pltpu.CompilerParams(dimension_semantics=("arbitrary",) * 3,
                                             vmem_limit_bytes=VMEM_LIMIT),
        name="attention",
    )(q, gates, kc, vc, ks, vs, kw, vw, overlap_t)


def _tail_kernel(h1_ref, on_ref, p_ref, nm_ref, wc_ref, wga_ref, wgb_ref, cw_ref, cb_ref,
                 wa_ref, wb_ref, wo_ref, n2_ref, wg_ref, wu_ref, wd_ref,
                 np_ref, pg_ref, pp_ref, nf_ref, out_ref, carry_ref, *, tiles_per_seq):
    i = pl.program_id(0)
    h1 = h1_ref[...]
    tm = h1.shape[0]
    u = _rms(h1, nm_ref[...]).astype(BF16)

    zc = _dot(u, wc_ref[...])
    gate_b = zc[:, :CONV_WIDTH]
    v = zc[:, CONV_WIDTH:2 * CONV_WIDTH] * zc[:, 2 * CONV_WIDTH:]

    @pl.when(i % tiles_per_seq == 0)
    def _():
        carry_ref[...] = jnp.zeros_like(carry_ref)

    prev = carry_ref[...]
    row = lax.broadcasted_iota(jnp.int32, (tm, CONV_WIDTH), 0)
    v1 = jnp.where(row == 0, prev[SUBLANES - 1:SUBLANES], pltpu.roll(v, 1, 0))
    v2 = jnp.where(row == 0, prev[SUBLANES - 2:SUBLANES - 1],
                   jnp.where(row == 1, prev[SUBLANES - 1:SUBLANES], pltpu.roll(v, 2, 0)))
    carry_ref[...] = v[tm - SUBLANES:]
    cw = cw_ref[...]
    y = cw[0:1] * v2 + cw[1:2] * v1 + cw[2:3] * v
    o_conv = (gate_b * (y + cb_ref[...])).astype(BF16)

    merged = (jax.nn.sigmoid(_dot(u, wga_ref[...])) * _dot(on_ref[...], wa_ref[...])
              + jax.nn.sigmoid(_dot(u, wgb_ref[...])) * _dot(o_conv, wb_ref[...]))
    h2 = h1 + _dot(merged.astype(BF16), wo_ref[...])

    h3 = h2 + 0.5 * _swiglu_half(_rms(h2, n2_ref[...]).astype(BF16), wg_ref, wu_ref, wd_ref)

    gate_p = jax.nn.sigmoid(_dot(_rms(h3, np_ref[...]).astype(BF16), pg_ref[...]))
    h4 = h3 + gate_p * _dot(p_ref[...].astype(BF16), pp_ref[...])
    out_ref[...] = _rms(h4, nf_ref[...])


def _tail_call(h1, o_nsa, p2, seq, nm, wc, wga, wgb, cw, cb, wa, wb, wo, n2, wg, wu, wd, npl, pg, pp, nf):
    t = h1.shape[0]
    tm = TAIL_TM
    assert CONV_K - 1 <= SUBLANES and seq % tm == 0

    def row(width):
        return pl.BlockSpec((tm, width), lambda i: (i, 0))

    consts = [nm, wc, wga, wgb, cw, cb, wa, wb, wo, n2, wg, wu, wd, npl, pg, pp, nf]
    return pl.pallas_call(
        functools.partial(_tail_kernel, tiles_per_seq=seq // tm),
        grid=(t // tm,),
        in_specs=[row(D_MODEL), row(NSA_WIDTH), row(PLE_DIM)] + [_const_spec(c.shape) for c in consts],
        out_specs=row(D_MODEL),
        out_shape=jax.ShapeDtypeStruct((t, D_MODEL), F32),
        scratch_shapes=[pltpu.VMEM((SUBLANES, CONV_WIDTH), F32)],
        compiler_params=pltpu.CompilerParams(dimension_semantics=("arbitrary",),
                                             vmem_limit_bytes=VMEM_LIMIT),
        name="tail",
    )(h1, o_nsa, p2, *consts)


def _compress_params(pos, w1, b1, w2):
    eye = jnp.eye(NSA_KV_HEADS, dtype=F32)
    half = CMP_LEN // 2
    w1r = w1.reshape(CMP_LEN, HEAD_DIM, CMP_HIDDEN)

    def expand(w):
        return jnp.einsum('ldn,gh->lgdhn', w, eye).reshape(half * KV_WIDTH, NSA_KV_HEADS * CMP_HIDDEN).astype(BF16)

    def pos_row(pp):
        return jnp.broadcast_to(pp[:, None, :], (half, NSA_KV_HEADS, HEAD_DIM)).reshape(1, half * KV_WIDTH)

    w2p = jnp.pad(w2, ((0, 0), (0, LANES - HEAD_DIM)))
    w2b = jnp.einsum('nd,gh->gnhd', w2p, eye).reshape(NSA_KV_HEADS * CMP_HIDDEN, NSA_KV_HEADS * LANES).astype(BF16)
    b1t = jnp.tile(b1, NSA_KV_HEADS).reshape(1, NSA_KV_HEADS * CMP_HIDDEN)
    return (pos_row(pos[:half]), pos_row(pos[half:]), expand(w1r[:half]), expand(w1r[half:]), b1t, w2b)


def kernel(x, p, positions, ffn1_norm, ffn1_w_gate, ffn1_w_up, ffn1_w_down, mix_norm, w_in, cmp_k_pos, cmp_k_w1, cmp_k_b1, cmp_k_w2, cmp_v_pos, cmp_v_w1, cmp_v_b1, cmp_v_w2, conv_w, conv_b, w_proj_nsa, w_proj_conv, w_out, ffn2_norm, ffn2_w_gate, ffn2_w_up, ffn2_w_down, ple_norm, ple_w_gate, ple_w_proj, final_norm):
    b, s, d = x.shape
    t = b * s
    depth = ffn1_norm.shape[0]
    assert depth == 1 and d == D_MODEL and s % SEL_CHUNK == 0

    half = HEAD_DIM // 2
    inv_freq = ROPE_THETA ** (-jnp.arange(half, dtype=F32) / half)
    invf = jnp.tile(inv_freq, LANES // half).reshape(1, LANES)
    pos = positions.astype(F32).reshape(t, 1)

    n_cmp_rows = s // CMP_STRIDE
    n_blk = s // SEL_LEN
    c_start = jnp.arange(n_cmp_rows) * CMP_STRIDE
    s_start = jnp.arange(n_blk) * SEL_LEN
    overlap_t = (jnp.clip(jnp.minimum(c_start[None, :] + CMP_LEN, s_start[:, None] + SEL_LEN)
                          - jnp.maximum(c_start[None, :], s_start[:, None]), 0, None).astype(F32)
                 / CMP_LEN).astype(BF16)

    row = lambda v: v.reshape(1, -1)
    h = x.reshape(t, d)
    for li in range(depth):
        wi = w_in[li]
        o = 0
        parts = []
        for sz in (NSA_WIDTH, 6 * KV_WIDTH, 3 * NSA_HEADS, 3 * CONV_WIDTH, D_MODEL, D_MODEL):
            parts.append(wi[:, o:o + sz])
            o += sz
        wq, wkv, wgn, wc, wga, wgb = parts
        per_group = 3 * HPG
        wgn = jnp.pad(wgn.reshape(d, NSA_KV_HEADS, per_group),
                      ((0, 0), (0, 0), (0, GATE_PAD - per_group))).reshape(d, NSA_KV_HEADS * GATE_PAD)

        h1, q, kc_r, vc_r, ks, vs, kw, vw, gates = _head_call(
            h, pos, invf, row(ffn1_norm[li]), ffn1_w_gate[li].astype(BF16), ffn1_w_up[li].astype(BF16),
            ffn1_w_down[li].astype(BF16), row(mix_norm[li]), wq.astype(BF16), wkv.astype(BF16),
            wgn.astype(BF16), s)

        chunk = CMP_STRIDE * KV_WIDTH
        kc, vc = _compress_call(
            kc_r.reshape(b, n_cmp_rows, chunk), vc_r.reshape(b, n_cmp_rows, chunk),
            _compress_params(cmp_k_pos[li], cmp_k_w1[li], cmp_k_b1[li], cmp_k_w2[li]),
            _compress_params(cmp_v_pos[li], cmp_v_w1[li], cmp_v_b1[li], cmp_v_w2[li]))

        seq3 = lambda a: a.reshape(b, s, a.shape[-1])
        o_nsa = _attn_call(seq3(q), seq3(gates), kc, vc, seq3(ks), seq3(vs), seq3(kw), seq3(vw), overlap_t)

        h = _tail_call(
            h1, o_nsa.reshape(t, NSA_WIDTH), p[li].reshape(t, PLE_DIM), s,
            row(mix_norm[li]), wc.astype(BF16), wga.astype(BF16), wgb.astype(BF16),
            conv_w[li], row(conv_b[li]), w_proj_nsa[li].astype(BF16), w_proj_conv[li].astype(BF16),
            w_out[li].astype(BF16), row(ffn2_norm[li]), ffn2_w_gate[li].astype(BF16),
            ffn2_w_up[li].astype(BF16), ffn2_w_down[li].astype(BF16), row(ple_norm[li]),
            ple_w_gate[li].astype(BF16), ple_w_proj[li].astype(BF16), row(final_norm))
    return h.reshape(b, s, d)
```

```python
import functools
import math

import jax
import jax.numpy as jnp
from jax import lax
from jax.experimental import pallas as pl
from jax.experimental.pallas import tpu as pltpu

D_MODEL = 1024
D_FF = 2816
PLE_DIM = 256
EPS = 1e-6
ROPE_THETA = 10000.0

NSA_HEADS = 8
NSA_KV_HEADS = 2
HEAD_DIM = 64
HPG = NSA_HEADS // NSA_KV_HEADS
NSA_WIDTH = NSA_HEADS * HEAD_DIM
KV_WIDTH = NSA_KV_HEADS * HEAD_DIM
CMP_LEN = 32
CMP_STRIDE = 16
CMP_HIDDEN = 256
SEL_LEN = 64
N_SEL = 16
N_LOCAL = 2
WINDOW = 512
Q_BLOCK = 128
FORCED_SCORE = 1e4
INVALID_SCORE = -1e4
CONV_WIDTH = 512
CONV_K = 3

SEL_SHIFT = SEL_LEN.bit_length() - 1

LANES = 128
SUBLANES = 8
VMEM_LIMIT = 58 * 1024 * 1024

HEAD_TM = 512
TAIL_TM = 256
FF_CHUNK = 1408
SEL_CHUNK = 512
GATE_PAD = LANES
BIAS_LANES = LANES - HEAD_DIM
NEG = -(2.0 ** 100)
Q_SCALE = HEAD_DIM ** -0.5 * math.log2(math.e)

BF16 = jnp.bfloat16
F32 = jnp.float32


def _rms(x, g):
    return x * lax.rsqrt(jnp.mean(x * x, axis=-1, keepdims=True) + EPS) * g


def _dot(a, b):
    return jnp.dot(a, b, preferred_element_type=F32)


def _dot_nt(a, b):
    return lax.dot_general(a, b, (((1,), (1,)), ((), ())), preferred_element_type=F32)


def _swiglu_half(xn, wg_ref, wu_ref, wd_ref):
    acc = jnp.zeros((xn.shape[0], D_MODEL), F32)
    for c in range(D_FF // FF_CHUNK):
        sl = slice(c * FF_CHUNK, (c + 1) * FF_CHUNK)
        gate = _dot(xn, wg_ref[:, sl])
        up = _dot(xn, wu_ref[:, sl])
        act = (gate * jax.nn.sigmoid(gate) * up).astype(BF16)
        acc = acc + _dot(act, wd_ref[sl, :])
    return acc


def _head_kernel(x_ref, pos_ref, invf_ref, n1_ref, wg_ref, wu_ref, wd_ref, n2_ref,
                 wq_ref, wkv_ref, wgn_ref,
                 h1_ref, q_ref, kc_ref, vc_ref, ks_ref, vs_ref, kw_ref, vw_ref, gate_ref, *, seq):
    x = x_ref[...]
    xn = _rms(x, n1_ref[...]).astype(BF16)
    h1 = x + 0.5 * _swiglu_half(xn, wg_ref, wu_ref, wd_ref)
    h1_ref[...] = h1
    u = _rms(h1, n2_ref[...]).astype(BF16)

    tm = x.shape[0]
    ang = pos_ref[...] * invf_ref[...]

    def low_half(width):
        lane = lax.broadcasted_iota(jnp.int32, (tm, width), 1)
        return (lane & (HEAD_DIM - 1)) < (HEAD_DIM // 2)

    cos = jnp.cos(ang)
    sin = jnp.sin(ang)
    sin_signed = jnp.where(low_half(LANES), -sin, sin)

    def rope(z):
        width = z.shape[1]
        reps = width // LANES
        c = jnp.concatenate([cos] * reps, axis=1) if reps > 1 else cos
        s = jnp.concatenate([sin_signed] * reps, axis=1) if reps > 1 else sin_signed
        partner = jnp.where(low_half(width), pltpu.roll(z, width - HEAD_DIM // 2, 1),
                            pltpu.roll(z, HEAD_DIM // 2, 1))
        return z * c + partner * s

    zq = _dot(u, wq_ref[...])
    q_ref[...] = (rope(zq) * Q_SCALE).astype(BF16)

    lane = lax.broadcasted_iota(jnp.int32, (tm, LANES), 1)
    is_dim = lane < HEAD_DIM
    ones_col = jnp.where(lane == HEAD_DIM, 1.0, 0.0)
    seq_pos = (pl.program_id(0) * tm + lax.broadcasted_iota(jnp.int32, (tm, LANES), 0)) & (seq - 1)
    blk_onehot = jnp.where(lane - HEAD_DIM == ((seq_pos >> SEL_SHIFT) & (BIAS_LANES - 1)), 1.0, 0.0)

    def per_group(z, extra):
        g0 = jnp.where(is_dim, z, extra)
        g1 = jnp.where(is_dim, pltpu.roll(z, HEAD_DIM, 1), extra)
        return jnp.concatenate([g0, g1], axis=1).astype(BF16)

    zkv = _dot(u, wkv_ref[...])
    kc_ref[...] = rope(zkv[:, 0 * KV_WIDTH:1 * KV_WIDTH]).astype(BF16)
    vc_ref[...] = zkv[:, 1 * KV_WIDTH:2 * KV_WIDTH].astype(BF16)
    ks_ref[...] = per_group(rope(zkv[:, 2 * KV_WIDTH:3 * KV_WIDTH]), blk_onehot)
    vs_ref[...] = per_group(zkv[:, 3 * KV_WIDTH:4 * KV_WIDTH], ones_col)
    kw_ref[...] = per_group(rope(zkv[:, 4 * KV_WIDTH:5 * KV_WIDTH]), 0.0)
    vw_ref[...] = per_group(zkv[:, 5 * KV_WIDTH:6 * KV_WIDTH], ones_col)
    gate_ref[...] = jax.nn.sigmoid(_dot(u, wgn_ref[...]))


def _const_spec(shape):
    nd = len(shape)
    return pl.BlockSpec(shape, lambda *_: (0,) * nd, pipeline_mode=pl.Buffered(1))


def _head_call(x2, pos, invf, n1, wg, wu, wd, n2, wq, wkv, wgn, seq):
    t = x2.shape[0]
    tm = HEAD_TM
    assert seq % tm == 0 and seq & (seq - 1) == 0

    def row(width):
        return pl.BlockSpec((tm, width), lambda i: (i, 0))

    def out(width, dtype):
        return jax.ShapeDtypeStruct((t, width), dtype)

    wide = NSA_KV_HEADS * LANES
    return pl.pallas_call(
        functools.partial(_head_kernel, seq=seq),
        grid=(t // tm,),
        in_specs=[row(D_MODEL), row(1), _const_spec((1, LANES)), _const_spec((1, D_MODEL)),
                  _const_spec((D_MODEL, D_FF)), _const_spec((D_MODEL, D_FF)), _const_spec((D_FF, D_MODEL)),
                  _const_spec((1, D_MODEL)), _const_spec((D_MODEL, NSA_WIDTH)),
                  _const_spec((D_MODEL, 6 * KV_WIDTH)), _const_spec((D_MODEL, NSA_KV_HEADS * GATE_PAD))],
        out_specs=[row(D_MODEL), row(NSA_WIDTH), row(KV_WIDTH), row(KV_WIDTH)] + [row(wide)] * 4
                  + [row(NSA_KV_HEADS * GATE_PAD)],
        out_shape=[out(D_MODEL, F32), out(NSA_WIDTH, BF16), out(KV_WIDTH, BF16), out(KV_WIDTH, BF16)]
                  + [out(wide, BF16)] * 4 + [out(NSA_KV_HEADS * GATE_PAD, F32)],
        compiler_params=pltpu.CompilerParams(dimension_semantics=("arbitrary",),
                                             vmem_limit_bytes=VMEM_LIMIT),
        name="head",
    )(x2, pos, invf, n1, wg, wu, wd, n2, wq, wkv, wgn)


def _compress_kernel(kx_ref, vx_ref,
                     kpa_ref, kpb_ref, kwa_ref, kwb_ref, kb1_ref, kw2_ref,
                     vpa_ref, vpb_ref, vwa_ref, vwb_ref, vb1_ref, vw2_ref,
                     kc_ref, vc_ref):
    def one(x_ref, pa_ref, pb_ref, wa_ref, wb_ref, b1_ref, w2_ref):
        x = x_ref[0].astype(F32)
        first = _dot((x + pa_ref[...]).astype(BF16), wa_ref[...])
        second = _dot((x + pb_ref[...]).astype(BF16), wb_ref[...])
        n = first.shape[0]
        hidden = first + pltpu.roll(second, n - 1, 0) + b1_ref[...]
        return _dot(jax.nn.gelu(hidden).astype(BF16), w2_ref[...])

    kc_ref[0] = one(kx_ref, kpa_ref, kpb_ref, kwa_ref, kwb_ref, kb1_ref, kw2_ref).astype(BF16)
    vc_ref[0] = one(vx_ref, vpa_ref, vpb_ref, vwa_ref, vwb_ref, vb1_ref, vw2_ref).astype(BF16)


def _compress_call(kx, vx, kparams, vparams):
    b, nrow, width = kx.shape
    hid = NSA_KV_HEADS * CMP_HIDDEN
    wide = NSA_KV_HEADS * LANES
    x_spec = pl.BlockSpec((1, nrow, width), lambda i: (i, 0, 0))
    w_specs = [_const_spec((1, width)), _const_spec((1, width)), _const_spec((width, hid)),
               _const_spec((width, hid)), _const_spec((1, hid)), _const_spec((hid, wide))]
    o_spec = pl.BlockSpec((1, nrow, wide), lambda i: (i, 0, 0))
    o_shape = jax.ShapeDtypeStruct((b, nrow, wide), BF16)
    return pl.pallas_call(
        _compress_kernel,
        grid=(b,),
        in_specs=[x_spec, x_spec] + w_specs + w_specs,
        out_specs=[o_spec, o_spec],
        out_shape=[o_shape, o_shape],
        compiler_params=pltpu.CompilerParams(dimension_semantics=("arbitrary",),
                                             vmem_limit_bytes=VMEM_LIMIT),
        name="compress",
    )(kx, vx, *kparams, *vparams)


def _attn_kernel(q_ref, gate_ref, kc_ref, vc_ref, ks_ref, vs_ref, kw_ref, vw_ref, ovl_ref, o_ref, qa_ref):
    qb = Q_BLOCK
    rows = HPG * qb
    qs = pl.program_id(2) * qb
    n_cmp = kc_ref.shape[1]
    n_blk = ovl_ref.shape[0]

    lane = lax.broadcasted_iota(jnp.int32, (qb, LANES), 1)
    is_dim = lane < HEAD_DIM
    tq = qs + lax.broadcasted_iota(jnp.int32, (qb, 1), 0)
    tq_row = qs + lax.broadcasted_iota(jnp.int32, (1, qb), 1)

    q_all = q_ref[0].astype(F32)
    q_heads = []
    for hh in range(HPG):
        pair = q_all[:, (hh // 2) * LANES:(hh // 2 + 1) * LANES]
        q_heads.append(pair if hh % 2 == 0 else pltpu.roll(pair, HEAD_DIM, 1))
    qg = jnp.concatenate([jnp.where(is_dim, qh, 0.0) for qh in q_heads], axis=0).astype(BF16)

    def mask_heads(mask, s):
        s3 = s.reshape(HPG, qb, s.shape[-1])
        return jnp.where(mask[None], s3, NEG).reshape(rows, s.shape[-1])

    def normalise(acc):
        return acc * (1.0 / acc[:, HEAD_DIM:HEAD_DIM + 1])

    c_last = lax.broadcasted_iota(jnp.int32, (qb, n_cmp), 1) * CMP_STRIDE + (CMP_LEN - 1)
    s_c = mask_heads(c_last <= tq, _dot_nt(qg, kc_ref[0]))
    span = WINDOW + qb
    w_start = pl.multiple_of(jnp.maximum(qs - WINDOW, 0), qb)
    kpos_w = w_start + lax.broadcasted_iota(jnp.int32, (qb, span), 1)
    s_w = mask_heads((kpos_w <= tq) & (kpos_w > tq - WINDOW), _dot_nt(qg, kw_ref[0, pl.ds(w_start, span), :]))

    e_c = jnp.exp2(s_c - jnp.max(s_c, axis=-1, keepdims=True)).reshape(HPG, qb, n_cmp)
    r_c = jnp.where(tq >= CMP_LEN - 1, 1.0, 0.0)[None] / jnp.sum(e_c, axis=-1, keepdims=True)
    p_c = e_c * r_c
    o_c = _dot(p_c.reshape(rows, n_cmp).astype(BF16), vc_ref[0])

    e_w = jnp.exp2(s_w - jnp.max(s_w, axis=-1, keepdims=True)).astype(BF16)
    o_w = normalise(_dot(e_w, vw_ref[0, pl.ds(w_start, span), :]))

    p_sum = jnp.sum(p_c, axis=0)
    p_hi = p_sum.astype(BF16)
    p_lo = (p_sum - p_hi.astype(F32)).astype(BF16)
    ovl = ovl_ref[...]
    imp = _dot_nt(ovl, p_hi) + _dot_nt(ovl, p_lo)
    s_idx = lax.broadcasted_iota(jnp.int32, (n_blk, qb), 0)
    blk_q = tq_row >> SEL_SHIFT
    valid = s_idx <= blk_q
    forced = (s_idx == 0) | (valid & (s_idx > blk_q - N_LOCAL))
    score = jnp.where(forced, -jnp.inf, jnp.where(valid, imp, INVALID_SCORE))
    chosen = forced
    s_idx_f = s_idx.astype(F32)
    for _ in range(N_SEL - 1 - N_LOCAL):
        top = jnp.max(score, axis=0, keepdims=True)
        first = jnp.min(jnp.where(score == top, s_idx_f, float(n_blk)), axis=0, keepdims=True)
        hit = s_idx_f == first
        chosen = chosen | hit
        score = jnp.where(hit, -jnp.inf, score)
    bias = jnp.where(chosen & valid, 0.0, NEG).T

    for w in range(n_blk // BIAS_LANES):
        b_w = bias[:, w * BIAS_LANES:(w + 1) * BIAS_LANES]
        b_w = jnp.concatenate([b_w, b_w], axis=1)
        qa_ref[w] = jnp.concatenate([jnp.where(is_dim, qh, b_w) for qh in q_heads], axis=0).astype(BF16)

    ck = SEL_CHUNK
    pairs_per_bias = BIAS_LANES * SEL_LEN // (2 * ck)

    def sel_pair(t, carry, causal):
        qa = qa_ref[t // pairs_per_bias]
        scores = []
        for half in range(2):
            start = pl.multiple_of((2 * t + half) * ck, ck)
            s = _dot_nt(qa, ks_ref[0, pl.ds(start, ck), :])
            if causal:
                s = mask_heads(start + lax.broadcasted_iota(jnp.int32, (qb, ck), 1) <= tq, s)
            scores.append((s, start))
        m_prev, acc = carry
        for s, start in scores:
            m_new = jnp.maximum(m_prev, jnp.max(s, axis=-1, keepdims=True))
            pv = _dot(jnp.exp2(s - m_new).astype(BF16), vs_ref[0, pl.ds(start, ck), :])
            m_prev, acc = m_new, jnp.exp2(m_prev - m_new) * acc + pv
        return m_prev, acc

    n_pairs = ((qs + qb - 1) // ck) // 2 + 1
    carry = (jnp.full((rows, 1), NEG, F32), jnp.zeros((rows, LANES), F32))
    carry = lax.fori_loop(0, n_pairs - 1, functools.partial(sel_pair, causal=False), carry)
    o_s = normalise(sel_pair(n_pairs - 1, carry, True)[1])

    gates = gate_ref[0]
    mixed = []
    for hh in range(HPG):
        sl = slice(hh * qb, (hh + 1) * qb)
        g_c = gates[:, 3 * hh + 0:3 * hh + 1]
        g_s = gates[:, 3 * hh + 1:3 * hh + 2]
        g_w = gates[:, 3 * hh + 2:3 * hh + 3]
        mixed.append(g_c * o_c[sl] + g_s * o_s[sl] + g_w * o_w[sl])
    for pr in range(HPG // 2):
        odd = pltpu.roll(mixed[2 * pr + 1], HEAD_DIM, 1)
        o_ref[0, :, pr * LANES:(pr + 1) * LANES] = jnp.where(is_dim, mixed[2 * pr], odd).astype(BF16)


def _attn_call(q, gates, kc, vc, ks, vs, kw, vw, overlap_t):
    b, s, _ = q.shape
    n_cmp = kc.shape[1]
    n_blk = s // SEL_LEN
    assert n_blk % BIAS_LANES == 0 and BIAS_LANES % (SEL_CHUNK // SEL_LEN) == 0
    gw = NSA_WIDTH // NSA_KV_HEADS
    seq_spec = pl.BlockSpec((1, s, LANES), lambda bi, g, i: (bi, 0, g))
    cmp_spec = pl.BlockSpec((1, n_cmp, LANES), lambda bi, g, i: (bi, 0, g))
    return pl.pallas_call(
        _attn_kernel,
        grid=(b, NSA_KV_HEADS, s // Q_BLOCK),
        in_specs=[pl.BlockSpec((1, Q_BLOCK, gw), lambda bi, g, i: (bi, i, g)),
                  pl.BlockSpec((1, Q_BLOCK, GATE_PAD), lambda bi, g, i: (bi, i, g)),
                  cmp_spec, cmp_spec, seq_spec, seq_spec, seq_spec, seq_spec,
                  _const_spec(overlap_t.shape)],
        out_specs=pl.BlockSpec((1, Q_BLOCK, gw), lambda bi, g, i: (bi, i, g)),
        out_shape=jax.ShapeDtypeStruct((b, s, NSA_WIDTH), BF16),
        scratch_shapes=[pltpu.VMEM((n_blk // BIAS_LANES, HPG * Q_BLOCK, LANES), BF16)],
        compiler_params=pltpu.CompilerParams(dimension_semantics=("arbitrary",) * 3,
                                             vmem_limit_bytes=VMEM_LIMIT),
        name="attention",
    )(q, gates, kc, vc, ks, vs, kw, vw, overlap_t)


def _tail_kernel(h1_ref, on_ref, p_ref, nm_ref, wc_ref, wga_ref, wgb_ref, cw_ref, cb_ref,
                 wa_ref, wb_ref, wo_ref, n2_ref, wg_ref, wu_ref, wd_ref,
                 np_ref, pg_ref, pp_ref, nf_ref, out_ref, carry_ref, *, tiles_per_seq):
    i = pl.program_id(0)
    h1 = h1_ref[...]
    tm = h1.shape[0]
    u = _rms(h1, nm_ref[...]).astype(BF16)

    zc = _dot(u, wc_ref[...])
    gate_b = zc[:, :CONV_WIDTH]
    v = zc[:, CONV_WIDTH:2 * CONV_WIDTH] * zc[:, 2 * CONV_WIDTH:]

    @pl.when(i % tiles_per_seq == 0)
    def _():
        carry_ref[...] = jnp.zeros_like(carry_ref)

    prev = carry_ref[...]
    row = lax.broadcasted_iota(jnp.int32, (tm, CONV_WIDTH), 0)
    v1 = jnp.where(row == 0, prev[SUBLANES - 1:SUBLANES], pltpu.roll(v, 1, 0))
    v2 = jnp.where(row == 0, prev[SUBLANES - 2:SUBLANES - 1],
                   jnp.where(row == 1, prev[SUBLANES - 1:SUBLANES], pltpu.roll(v, 2, 0)))
    carry_ref[...] = v[tm - SUBLANES:]
    cw = cw_ref[...]
    y = cw[0:1] * v2 + cw[1:2] * v1 + cw[2:3] * v
    o_conv = (gate_b * (y + cb_ref[...])).astype(BF16)

    merged = (jax.nn.sigmoid(_dot(u, wga_ref[...])) * _dot(on_ref[...], wa_ref[...])
              + jax.nn.sigmoid(_dot(u, wgb_ref[...])) * _dot(o_conv, wb_ref[...]))
    h2 = h1 + _dot(merged.astype(BF16), wo_ref[...])

    h3 = h2 + 0.5 * _swiglu_half(_rms(h2, n2_ref[...]).astype(BF16), wg_ref, wu_ref, wd_ref)

    gate_p = jax.nn.sigmoid(_dot(_rms(h3, np_ref[...]).astype(BF16), pg_ref[...]))
    h4 = h3 + gate_p * _dot(p_ref[...].astype(BF16), pp_ref[...])
    out_ref[...] = _rms(h4, nf_ref[...])


def _tail_call(h1, o_nsa, p2, seq, nm, wc, wga, wgb, cw, cb, wa, wb, wo, n2, wg, wu, wd, npl, pg, pp, nf):
    t = h1.shape[0]
    tm = TAIL_TM
    assert CONV_K - 1 <= SUBLANES and seq % tm == 0

    def row(width):
        return pl.BlockSpec((tm, width), lambda i: (i, 0))

    consts = [nm, wc, wga, wgb, cw, cb, wa, wb, wo, n2, wg, wu, wd, npl, pg, pp, nf]
    return pl.pallas_call(
        functools.partial(_tail_kernel, tiles_per_seq=seq // tm),
        grid=(t // tm,),
        in_specs=[row(D_MODEL), row(NSA_WIDTH), row(PLE_DIM)] + [_const_spec(c.shape) for c in consts],
        out_specs=row(D_MODEL),
        out_shape=jax.ShapeDtypeStruct((t, D_MODEL), F32),
        scratch_shapes=[pltpu.VMEM((SUBLANES, CONV_WIDTH), F32)],
        compiler_params=pltpu.CompilerParams(dimension_semantics=("arbitrary",),
                                             vmem_limit_bytes=VMEM_LIMIT),
        name="tail",
    )(h1, o_nsa, p2, *consts)


def _compress_params(pos, w1, b1, w2):
    eye = jnp.eye(NSA_KV_HEADS, dtype=F32)
    half = CMP_LEN // 2
    w1r = w1.reshape(CMP_LEN, HEAD_DIM, CMP_HIDDEN)

    def expand(w):
        return jnp.einsum('ldn,gh->lgdhn', w, eye).reshape(half * KV_WIDTH, NSA_KV_HEADS * CMP_HIDDEN).astype(BF16)

    def pos_row(pp):
        return jnp.broadcast_to(pp[:, None, :], (half, NSA_KV_HEADS, HEAD_DIM)).reshape(1, half * KV_WIDTH)

    w2p = jnp.pad(w2, ((0, 0), (0, LANES - HEAD_DIM)))
    w2b = jnp.einsum('nd,gh->gnhd', w2p, eye).reshape(NSA_KV_HEADS * CMP_HIDDEN, NSA_KV_HEADS * LANES).astype(BF16)
    b1t = jnp.tile(b1, NSA_KV_HEADS).reshape(1, NSA_KV_HEADS * CMP_HIDDEN)
    return (pos_row(pos[:half]), pos_row(pos[half:]), expand(w1r[:half]), expand(w1r[half:]), b1t, w2b)


def kernel(x, p, positions, ffn1_norm, ffn1_w_gate, ffn1_w_up, ffn1_w_down, mix_norm, w_in, cmp_k_pos, cmp_k_w1, cmp_k_b1, cmp_k_w2, cmp_v_pos, cmp_v_w1, cmp_v_b1, cmp_v_w2, conv_w, conv_b, w_proj_nsa, w_proj_conv, w_out, ffn2_norm, ffn2_w_gate, ffn2_w_up, ffn2_w_down, ple_norm, ple_w_gate, ple_w_proj, final_norm):
    b, s, d = x.shape
    t = b * s
    depth = ffn1_norm.shape[0]
    assert depth == 1 and d == D_MODEL and s % SEL_CHUNK == 0

    half = HEAD_DIM // 2
    inv_freq = ROPE_THETA ** (-jnp.arange(half, dtype=F32) / half)
    invf = jnp.tile(inv_freq, LANES // half).reshape(1, LANES)
    pos = positions.astype(F32).reshape(t, 1)

    n_cmp_rows = s // CMP_STRIDE
    n_blk = s // SEL_LEN
    c_start = jnp.arange(n_cmp_rows) * CMP_STRIDE
    s_start = jnp.arange(n_blk) * SEL_LEN
    overlap_t = (jnp.clip(jnp.minimum(c_start[None, :] + CMP_LEN, s_start[:, None] + SEL_LEN)
                          - jnp.maximum(c_start[None, :], s_start[:, None]), 0, None).astype(F32)
                 / CMP_LEN).astype(BF16)

    row = lambda v: v.reshape(1, -1)
    h = x.reshape(t, d)
    for li in range(depth):
        wi = w_in[li]
        o = 0
        parts = []
        for sz in (NSA_WIDTH, 6 * KV_WIDTH, 3 * NSA_HEADS, 3 * CONV_WIDTH, D_MODEL, D_MODEL):
            parts.append(wi[:, o:o + sz])
            o += sz
        wq, wkv, wgn, wc, wga, wgb = parts
        per_group = 3 * HPG
        wgn = jnp.pad(wgn.reshape(d, NSA_KV_HEADS, per_group),
                      ((0, 0), (0, 0), (0, GATE_PAD - per_group))).reshape(d, NSA_KV_HEADS * GATE_PAD)

        h1, q, kc_r, vc_r, ks, vs, kw, vw, gates = _head_call(
            h, pos, invf, row(ffn1_norm[li]), ffn1_w_gate[li].astype(BF16), ffn1_w_up[li].astype(BF16),
            ffn1_w_down[li].astype(BF16), row(mix_norm[li]), wq.astype(BF16), wkv.astype(BF16),
            wgn.astype(BF16), s)

        chunk = CMP_STRIDE * KV_WIDTH
        kc, vc = _compress_call(
            kc_r.reshape(b, n_cmp_rows, chunk), vc_r.reshape(b, n_cmp_rows, chunk),
            _compress_params(cmp_k_pos[li], cmp_k_w1[li], cmp_k_b1[li], cmp_k_w2[li]),
            _compress_params(cmp_v_pos[li], cmp_v_w1[li], cmp_v_b1[li], cmp_v_w2[li]))

        seq3 = lambda a: a.reshape(b, s, a.shape[-1])
        o_nsa = _attn_call(seq3(q), seq3(gates), kc, vc, seq3(ks), seq3(vs), seq3(kw), seq3(vw), overlap_t)

        h = _tail_call(
            h1, o_nsa.reshape(t, NSA_WIDTH), p[li].reshape(t, PLE_DIM), s,
            row(mix_norm[li]), wc.astype(BF16), wga.astype(BF16), wgb.astype(BF16),
            conv_w[li], row(conv_b[li]), w_proj_nsa[li].astype(BF16), w_proj_conv[li].astype(BF16),
            w_out[li].astype(BF16), row(ffn2_norm[li]), ffn2_w_gate[li].astype(BF16),
            ffn2_w_up[li].astype(BF16), ffn2_w_down[li].astype(BF16), row(ple_norm[li]),
            ple_w_gate[li].astype(BF16), ple_w_proj[li].astype(BF16), row(final_norm))
    return h.reshape(b, s, d)
```

```python
import functools
import math

import jax
import jax.numpy as jnp
from jax import lax
from jax.experimental import pallas as pl
from jax.experimental.pallas import tpu as pltpu

D_MODEL = 1024
D_FF = 2816
PLE_DIM = 256
EPS = 1e-6
ROPE_THETA = 10000.0

NSA_HEADS = 8
NSA_KV_HEADS = 2
HEAD_DIM = 64
HPG = NSA_HEADS // NSA_KV_HEADS
NSA_WIDTH = NSA_HEADS * HEAD_DIM
KV_WIDTH = NSA_KV_HEADS * HEAD_DIM
CMP_LEN = 32
CMP_STRIDE = 16
CMP_HIDDEN = 256
SEL_LEN = 64
N_SEL = 16
N_LOCAL = 2
WINDOW = 512
Q_BLOCK = 128
FORCED_SCORE = 1e4
INVALID_SCORE = -1e4
CONV_WIDTH = 512
CONV_K = 3

SEL_SHIFT = SEL_LEN.bit_length() - 1

LANES = 128
SUBLANES = 8
VMEM_LIMIT = 58 * 1024 * 1024

HEAD_TM = 512
TAIL_TM = 256
FF_CHUNK = 1408
SEL_CHUNK = 512
SEL_SPAN_CHUNKS = 2
SEL_UNROLL = 2
GATE_PAD = LANES
BIAS_LANES = LANES - HEAD_DIM
NEG = -(2.0 ** 100)
Q_SCALE = HEAD_DIM ** -0.5 * math.log2(math.e)

BF16 = jnp.bfloat16
F32 = jnp.float32


def _rms(x, g):
    return x * lax.rsqrt(jnp.mean(x * x, axis=-1, keepdims=True) + EPS) * g


def _dot(a, b):
    return jnp.dot(a, b, preferred_element_type=F32)


def _dot_nt(a, b):
    return lax.dot_general(a, b, (((1,), (1,)), ((), ())), preferred_element_type=F32)


def _swiglu_half(xn, wg_ref, wu_ref, wd_ref):
    acc = jnp.zeros((xn.shape[0], D_MODEL), F32)
    for c in range(D_FF // FF_CHUNK):
        sl = slice(c * FF_CHUNK, (c + 1) * FF_CHUNK)
        gate = _dot(xn, wg_ref[:, sl])
        up = _dot(xn, wu_ref[:, sl])
        act = (gate * jax.nn.sigmoid(gate) * up).astype(BF16)
        acc = acc + _dot(act, wd_ref[sl, :])
    return acc


def _head_kernel(x_ref, pos_ref, invf_ref, n1_ref, wg_ref, wu_ref, wd_ref, n2_ref,
                 wq_ref, wkv_ref, wgn_ref,
                 h1_ref, q_ref, kc_ref, vc_ref, ks_ref, vs_ref, kw_ref, vw_ref, gate_ref, *, seq):
    x = x_ref[...]
    xn = _rms(x, n1_ref[...]).astype(BF16)
    h1 = x + 0.5 * _swiglu_half(xn, wg_ref, wu_ref, wd_ref)
    h1_ref[...] = h1
    u = _rms(h1, n2_ref[...]).astype(BF16)

    tm = x.shape[0]
    ang = pos_ref[...] * invf_ref[...]

    def low_half(width):
        lane = lax.broadcasted_iota(jnp.int32, (tm, width), 1)
        return (lane & (HEAD_DIM - 1)) < (HEAD_DIM // 2)

    cos = jnp.cos(ang)
    sin = jnp.sin(ang)
    sin_signed = jnp.where(low_half(LANES), -sin, sin)

    def rope(z):
        width = z.shape[1]
        reps = width // LANES
        c = jnp.concatenate([cos] * reps, axis=1) if reps > 1 else cos
        s = jnp.concatenate([sin_signed] * reps, axis=1) if reps > 1 else sin_signed
        partner = jnp.where(low_half(width), pltpu.roll(z, width - HEAD_DIM // 2, 1),
                            pltpu.roll(z, HEAD_DIM // 2, 1))
        return z * c + partner * s

    zq = _dot(u, wq_ref[...])
    q_ref[...] = (rope(zq) * Q_SCALE).astype(BF16)

    lane = lax.broadcasted_iota(jnp.int32, (tm, LANES), 1)
    is_dim = lane < HEAD_DIM
    ones_col = jnp.where(lane == HEAD_DIM, 1.0, 0.0)
    seq_pos = (pl.program_id(0) * tm + lax.broadcasted_iota(jnp.int32, (tm, LANES), 0)) & (seq - 1)
    blk_onehot = jnp.where(lane - HEAD_DIM == ((seq_pos >> SEL_SHIFT) & (BIAS_LANES - 1)), 1.0, 0.0)

    def per_group(z, extra):
        g0 = jnp.where(is_dim, z, extra)
        g1 = jnp.where(is_dim, pltpu.roll(z, HEAD_DIM, 1), extra)
        return jnp.concatenate([g0, g1], axis=1).astype(BF16)

    zkv = _dot(u, wkv_ref[...])
    kc_ref[...] = rope(zkv[:, 0 * KV_WIDTH:1 * KV_WIDTH]).astype(BF16)
    vc_ref[...] = zkv[:, 1 * KV_WIDTH:2 * KV_WIDTH].astype(BF16)
    ks_ref[...] = per_group(rope(zkv[:, 2 * KV_WIDTH:3 * KV_WIDTH]), blk_onehot)
    vs_ref[...] = per_group(zkv[:, 3 * KV_WIDTH:4 * KV_WIDTH], ones_col)
    kw_ref[...] = per_group(rope(zkv[:, 4 * KV_WIDTH:5 * KV_WIDTH]), 0.0)
    vw_ref[...] = per_group(zkv[:, 5 * KV_WIDTH:6 * KV_WIDTH], ones_col)
    gate_ref[...] = jax.nn.sigmoid(_dot(u, wgn_ref[...]))


def _const_spec(shape):
    nd = len(shape)
    return pl.BlockSpec(shape, lambda *_: (0,) * nd, pipeline_mode=pl.Buffered(1))


def _head_call(x2, pos, invf, n1, wg, wu, wd, n2, wq, wkv, wgn, seq):
    t = x2.shape[0]
    tm = HEAD_TM
    assert seq % tm == 0 and seq & (seq - 1) == 0

    def row(width):
        return pl.BlockSpec((tm, width), lambda i: (i, 0))

    def out(width, dtype):
        return jax.ShapeDtypeStruct((t, width), dtype)

    wide = NSA_KV_HEADS * LANES
    return pl.pallas_call(
        functools.partial(_head_kernel, seq=seq),
        grid=(t // tm,),
        in_specs=[row(D_MODEL), row(1), _const_spec((1, LANES)), _const_spec((1, D_MODEL)),
                  _const_spec((D_MODEL, D_FF)), _const_spec((D_MODEL, D_FF)), _const_spec((D_FF, D_MODEL)),
                  _const_spec((1, D_MODEL)), _const_spec((D_MODEL, NSA_WIDTH)),
                  _const_spec((D_MODEL, 6 * KV_WIDTH)), _const_spec((D_MODEL, NSA_KV_HEADS * GATE_PAD))],
        out_specs=[row(D_MODEL), row(NSA_WIDTH), row(KV_WIDTH), row(KV_WIDTH)] + [row(wide)] * 4
                  + [row(NSA_KV_HEADS * GATE_PAD)],
        out_shape=[out(D_MODEL, F32), out(NSA_WIDTH, BF16), out(KV_WIDTH, BF16), out(KV_WIDTH, BF16)]
                  + [out(wide, BF16)] * 4 + [out(NSA_KV_HEADS * GATE_PAD, F32)],
        compiler_params=pltpu.CompilerParams(dimension_semantics=("arbitrary",),
                                             vmem_limit_bytes=VMEM_LIMIT),
        name="head",
    )(x2, pos, invf, n1, wg, wu, wd, n2, wq, wkv, wgn)


def _compress_kernel(kx_ref, vx_ref,
                     kpa_ref, kpb_ref, kwa_ref, kwb_ref, kb1_ref, kw2_ref,
                     vpa_ref, vpb_ref, vwa_ref, vwb_ref, vb1_ref, vw2_ref,
                     kc_ref, vc_ref):
    def one(x_ref, pa_ref, pb_ref, wa_ref, wb_ref, b1_ref, w2_ref):
        x = x_ref[0].astype(F32)
        first = _dot((x + pa_ref[...]).astype(BF16), wa_ref[...])
        second = _dot((x + pb_ref[...]).astype(BF16), wb_ref[...])
        n = first.shape[0]
        hidden = first + pltpu.roll(second, n - 1, 0) + b1_ref[...]
        return _dot(jax.nn.gelu(hidden).astype(BF16), w2_ref[...])

    kc_ref[0] = one(kx_ref, kpa_ref, kpb_ref, kwa_ref, kwb_ref, kb1_ref, kw2_ref).astype(BF16)
    vc_ref[0] = one(vx_ref, vpa_ref, vpb_ref, vwa_ref, vwb_ref, vb1_ref, vw2_ref).astype(BF16)


def _compress_call(kx, vx, kparams, vparams):
    b, nrow, width = kx.shape
    hid = NSA_KV_HEADS * CMP_HIDDEN
    wide = NSA_KV_HEADS * LANES
    x_spec = pl.BlockSpec((1, nrow, width), lambda i: (i, 0, 0))
    w_specs = [_const_spec((1, width)), _const_spec((1, width)), _const_spec((width, hid)),
               _const_spec((width, hid)), _const_spec((1, hid)), _const_spec((hid, wide))]
    o_spec = pl.BlockSpec((1, nrow, wide), lambda i: (i, 0, 0))
    o_shape = jax.ShapeDtypeStruct((b, nrow, wide), BF16)
    return pl.pallas_call(
        _compress_kernel,
        grid=(b,),
        in_specs=[x_spec, x_spec] + w_specs + w_specs,
        out_specs=[o_spec, o_spec],
        out_shape=[o_shape, o_shape],
        compiler_params=pltpu.CompilerParams(dimension_semantics=("arbitrary",),
                                             vmem_limit_bytes=VMEM_LIMIT),
        name="compress",
    )(kx, vx, *kparams, *vparams)


def _attn_kernel(q_ref, qn_ref, gate_ref, kc_ref, vc_ref, ks_ref, vs_ref, kw_ref, vw_ref, ovl_ref, o_ref,
                 qa_ref, oc_ref):
    qb = Q_BLOCK
    rows = HPG * qb
    groups = range(NSA_KV_HEADS)
    step = pl.program_id(1)
    slot = step % 2
    n_cmp = kc_ref.shape[1]
    n_blk = ovl_ref.shape[0]

    def tile(g):
        return slice(g * LANES, (g + 1) * LANES)

    lane = lax.broadcasted_iota(jnp.int32, (qb, LANES), 1)
    is_dim = lane < HEAD_DIM

    def positions(blk):
        return blk * qb + lax.broadcasted_iota(jnp.int32, (qb, 1), 0)

    def split_heads(block_ref):
        q_all = block_ref[0].astype(F32)
        q_heads = []
        for g in groups:
            heads = []
            for hh in range(HPG):
                pair = q_all[:, (g * HPG + hh) // 2 * LANES:((g * HPG + hh) // 2 + 1) * LANES]
                heads.append(pair if hh % 2 == 0 else pltpu.roll(pair, HEAD_DIM, 1))
            q_heads.append(heads)
        return q_heads

    def stack_heads(heads):
        return jnp.concatenate([jnp.where(is_dim, qh, 0.0) for qh in heads], axis=0).astype(BF16)

    def mask_heads(mask, s):
        s3 = s.reshape(HPG, qb, s.shape[-1])
        return jnp.where(mask[None], s3, NEG).reshape(rows, s.shape[-1])

    def normalise(acc):
        return acc * (1.0 / acc[:, HEAD_DIM:HEAD_DIM + 1])

    def select(block_ref, blk, dst, fillers=()):
        fillers = list(fillers)
        tq = positions(blk)
        q_heads = split_heads(block_ref)

        c_last = lax.broadcasted_iota(jnp.int32, (qb, n_cmp), 1) * CMP_STRIDE + (CMP_LEN - 1)
        c_mask = c_last <= tq
        s_c = [mask_heads(c_mask, _dot_nt(stack_heads(q_heads[g]), kc_ref[0, :, tile(g)])) for g in groups]
        sees_cmp = jnp.where(tq >= CMP_LEN - 1, 1.0, 0.0)[None]
        p_c = []
        for g in groups:
            e_c = jnp.exp2(s_c[g] - jnp.max(s_c[g], axis=-1, keepdims=True)).reshape(HPG, qb, n_cmp)
            p_c.append(e_c * (sees_cmp / jnp.sum(e_c, axis=-1, keepdims=True)))
            oc_ref[dst, g] = _dot(p_c[g].reshape(rows, n_cmp).astype(BF16), vc_ref[0, :, tile(g)])

        ovl = ovl_ref[...]
        imp = []
        for g in groups:
            p_sum = jnp.sum(p_c[g], axis=0)
            p_hi = p_sum.astype(BF16)
            p_lo = (p_sum - p_hi.astype(F32)).astype(BF16)
            imp.append(_dot_nt(ovl, p_hi) + _dot_nt(ovl, p_lo))
        imp = jnp.concatenate(imp, axis=1)
        nq = NSA_KV_HEADS * qb
        s_idx = lax.broadcasted_iota(jnp.int32, (n_blk, nq), 0)
        tq_row = blk * qb + (lax.broadcasted_iota(jnp.int32, (1, nq), 1) & (qb - 1))
        blk_q = tq_row >> SEL_SHIFT
        valid = s_idx <= blk_q
        forced = (s_idx == 0) | (valid & (s_idx > blk_q - N_LOCAL))
        score = jnp.where(forced, -jnp.inf, jnp.where(valid, imp, INVALID_SCORE))
        chosen = forced
        s_idx_f = s_idx.astype(F32)
        n_rounds = N_SEL - 1 - N_LOCAL
        every = -(-n_rounds // len(fillers)) if fillers else 0
        for r in range(n_rounds):
            top = jnp.max(score, axis=0, keepdims=True)
            first = jnp.min(jnp.where(score == top, s_idx_f, float(n_blk)), axis=0, keepdims=True)
            hit = s_idx_f == first
            chosen = chosen | hit
            score = jnp.where(hit, -jnp.inf, score)
            if fillers and r % every == 0:
                fillers.pop(0)()
        for filler in fillers:
            filler()
        bias = jnp.where(chosen & valid, 0.0, NEG)

        for g in groups:
            bias_g = bias[:, g * qb:(g + 1) * qb].T
            for w in range(n_blk // BIAS_LANES):
                b_w = bias_g[:, w * BIAS_LANES:(w + 1) * BIAS_LANES]
                b_w = jnp.concatenate([b_w, b_w], axis=1)
                qa_ref[dst, g, w] = jnp.concatenate([jnp.where(is_dim, qh, b_w) for qh in q_heads[g]],
                                                    axis=0).astype(BF16)

    @pl.when(step == 0)
    def _():
        select(q_ref, 0, 0)

    qs = step * qb
    tq = positions(step)
    ck = SEL_CHUNK
    span_keys = SEL_SPAN_CHUNKS * ck
    spans_per_bias = BIAS_LANES * SEL_LEN // span_keys

    def sel_span(t, carry, n_spans, causal, which=groups):
        carry = list(carry)
        for g in which:
            m_prev, acc = carry[g]
            for u in range(n_spans):
                qa = qa_ref[slot, g, (n_spans * t + u) // spans_per_bias]
                scores = []
                for c in range(SEL_SPAN_CHUNKS):
                    start = pl.multiple_of((n_spans * t + u) * span_keys + c * ck, ck)
                    s = _dot_nt(qa, ks_ref[0, pl.ds(start, ck), tile(g)])
                    if causal:
                        s = mask_heads(start + lax.broadcasted_iota(jnp.int32, (qb, ck), 1) <= tq, s)
                    scores.append((s, start))
                for s, start in scores:
                    m_new = jnp.maximum(m_prev, jnp.max(s, axis=-1, keepdims=True))
                    pv = _dot(jnp.exp2(s - m_new).astype(BF16), vs_ref[0, pl.ds(start, ck), tile(g)])
                    m_prev, acc = m_new, jnp.exp2(m_prev - m_new) * acc + pv
            carry[g] = (m_prev, acc)
        return tuple(carry)

    n_plain = (qs + qb - 1) // span_keys
    carry = ((jnp.full((rows, 1), NEG, F32), jnp.zeros((rows, LANES), F32)),) * NSA_KV_HEADS
    n_wide = n_plain // SEL_UNROLL
    if SEL_UNROLL > 1:
        carry = lax.fori_loop(0, n_wide, functools.partial(sel_span, n_spans=SEL_UNROLL, causal=False), carry)
    carry = lax.fori_loop(n_wide * SEL_UNROLL, n_plain, functools.partial(sel_span, n_spans=1, causal=False),
                          carry)

    state = {"carry": carry, "o_w": [None] * NSA_KV_HEADS}
    span = WINDOW + qb
    w_start = pl.multiple_of(jnp.maximum(qs - WINDOW, 0), qb)
    q_heads = split_heads(q_ref)

    def sweep_end(g):
        def run():
            state["carry"] = sel_span(n_plain, state["carry"], 1, True, which=(g,))
        return run

    def window(g):
        def run():
            kpos_w = w_start + lax.broadcasted_iota(jnp.int32, (qb, span), 1)
            w_mask = (kpos_w <= tq) & (kpos_w > tq - WINDOW)
            s_w = mask_heads(w_mask, _dot_nt(stack_heads(q_heads[g]), kw_ref[0, pl.ds(w_start, span), tile(g)]))
            e_w = jnp.exp2(s_w - jnp.max(s_w, axis=-1, keepdims=True)).astype(BF16)
            state["o_w"][g] = normalise(_dot(e_w, vw_ref[0, pl.ds(w_start, span), tile(g)]))
        return run

    select(qn_ref, step + 1, 1 - slot,
           fillers=[sweep_end(g) for g in groups] + [window(g) for g in groups])
    o_s = [normalise(state["carry"][g][1]) for g in groups]
    o_w = state["o_w"]

    for g in groups:
        gates = gate_ref[0, :, tile(g)]
        o_c = oc_ref[slot, g]
        mixed = []
        for hh in range(HPG):
            sl = slice(hh * qb, (hh + 1) * qb)
            g_c = gates[:, 3 * hh + 0:3 * hh + 1]
            g_s = gates[:, 3 * hh + 1:3 * hh + 2]
            g_w = gates[:, 3 * hh + 2:3 * hh + 3]
            mixed.append(g_c * o_c[sl] + g_s * o_s[g][sl] + g_w * o_w[g][sl])
        for pr in range(HPG // 2):
            odd = pltpu.roll(mixed[2 * pr + 1], HEAD_DIM, 1)
            col = (g * HPG // 2 + pr) * LANES
            o_ref[0, :, col:col + LANES] = jnp.where(is_dim, mixed[2 * pr], odd).astype(BF16)


def _attn_call(q, gates, kc, vc, ks, vs, kw, vw, overlap_t):
    b, s, _ = q.shape
    n_cmp = kc.shape[1]
    n_blk = s // SEL_LEN
    span_keys = SEL_SPAN_CHUNKS * SEL_CHUNK
    assert n_blk % BIAS_LANES == 0 and BIAS_LANES * SEL_LEN % span_keys == 0 and s % span_keys == 0
    wide = NSA_KV_HEADS * LANES
    seq_spec = pl.BlockSpec((1, s, wide), lambda bi, i: (bi, 0, 0), pipeline_mode=pl.Buffered(1))
    cmp_spec = pl.BlockSpec((1, n_cmp, wide), lambda bi, i: (bi, 0, 0), pipeline_mode=pl.Buffered(1))
    n_steps = s // Q_BLOCK
    rows = HPG * Q_BLOCK
    return pl.pallas_call(
        _attn_kernel,
        grid=(b, n_steps),
        in_specs=[pl.BlockSpec((1, Q_BLOCK, NSA_WIDTH), lambda bi, i: (bi, i, 0)),
                  pl.BlockSpec((1, Q_BLOCK, NSA_WIDTH), lambda bi, i: (bi, jnp.minimum(i + 1, n_steps - 1), 0)),
                  pl.BlockSpec((1, Q_BLOCK, NSA_KV_HEADS * GATE_PAD), lambda bi, i: (bi, i, 0)),
                  cmp_spec, cmp_spec, seq_spec, seq_spec, seq_spec, seq_spec,
                  _const_spec(overlap_t.shape)],
        out_specs=pl.BlockSpec((1, Q_BLOCK, NSA_WIDTH), lambda bi, i: (bi, i, 0)),
        out_shape=jax.ShapeDtypeStruct((b, s, NSA_WIDTH), BF16),
        scratch_shapes=[pltpu.VMEM((2, NSA_KV_HEADS, n_blk // BIAS_LANES, rows, LANES), BF16),
                        pltpu.VMEM((2, NSA_KV_HEADS, rows, LANES), F32)],
        compiler_params=pltpu.CompilerParams(dimension_semantics=("arbitrary",) * 2,
                                             vmem_limit_bytes=VMEM_LIMIT),
        name="attention",
    )(q, q, gates, kc, vc, ks, vs, kw, vw, overlap_t)


def _tail_kernel(h1_ref, on_ref, p_ref, nm_ref, wc_ref, wga_ref, wgb_ref, cw_ref, cb_ref,
                 wa_ref, wb_ref, wo_ref, n2_ref, wg_ref, wu_ref, wd_ref,
                 np_ref, pg_ref, pp_ref, nf_ref, out_ref, carry_ref, *, tiles_per_seq):
    i = pl.program_id(0)
    h1 = h1_ref[...]
    tm = h1.shape[0]
    u = _rms(h1, nm_ref[...]).astype(BF16)

    zc = _dot(u, wc_ref[...])
    gate_b = zc[:, :CONV_WIDTH]
    v = zc[:, CONV_WIDTH:2 * CONV_WIDTH] * zc[:, 2 * CONV_WIDTH:]

    @pl.when(i % tiles_per_seq == 0)
    def _():
        carry_ref[...] = jnp.zeros_like(carry_ref)

    prev = carry_ref[...]
    row = lax.broadcasted_iota(jnp.int32, (tm, CONV_WIDTH), 0)
    v1 = jnp.where(row == 0, prev[SUBLANES - 1:SUBLANES], pltpu.roll(v, 1, 0))
    v2 = jnp.where(row == 0, prev[SUBLANES - 2:SUBLANES - 1],
                   jnp.where(row == 1, prev[SUBLANES - 1:SUBLANES], pltpu.roll(v, 2, 0)))
    carry_ref[...] = v[tm - SUBLANES:]
    cw = cw_ref[...]
    y = cw[0:1] * v2 + cw[1:2] * v1 + cw[2:3] * v
    o_conv = (gate_b * (y + cb_ref[...])).astype(BF16)

    merged = (jax.nn.sigmoid(_dot(u, wga_ref[...])) * _dot(on_ref[...], wa_ref[...])
              + jax.nn.sigmoid(_dot(u, wgb_ref[...])) * _dot(o_conv, wb_ref[...]))
    h2 = h1 + _dot(merged.astype(BF16), wo_ref[...])

    h3 = h2 + 0.5 * _swiglu_half(_rms(h2, n2_ref[...]).astype(BF16), wg_ref, wu_ref, wd_ref)

    gate_p = jax.nn.sigmoid(_dot(_rms(h3, np_ref[...]).astype(BF16), pg_ref[...]))
    h4 = h3 + gate_p * _dot(p_ref[...].astype(BF16), pp_ref[...])
    out_ref[...] = _rms(h4, nf_ref[...])


def _tail_call(h1, o_nsa, p2, seq, nm, wc, wga, wgb, cw, cb, wa, wb, wo, n2, wg, wu, wd, npl, pg, pp, nf):
    t = h1.shape[0]
    tm = TAIL_TM
    assert CONV_K - 1 <= SUBLANES and seq % tm == 0

    def row(width):
        return pl.BlockSpec((tm, width), lambda i: (i, 0))

    consts = [nm, wc, wga, wgb, cw, cb, wa, wb, wo, n2, wg, wu, wd, npl, pg, pp, nf]
    return pl.pallas_call(
        functools.partial(_tail_kernel, tiles_per_seq=seq // tm),
        grid=(t // tm,),
        in_specs=[row(D_MODEL), row(NSA_WIDTH), row(PLE_DIM)] + [_const_spec(c.shape) for c in consts],
        out_specs=row(D_MODEL),
        out_shape=jax.ShapeDtypeStruct((t, D_MODEL), F32),
        scratch_shapes=[pltpu.VMEM((SUBLANES, CONV_WIDTH), F32)],
        compiler_params=pltpu.CompilerParams(dimension_semantics=("arbitrary",),
                                             vmem_limit_bytes=VMEM_LIMIT),
        name="tail",
    )(h1, o_nsa, p2, *consts)


def _compress_params(pos, w1, b1, w2):
    eye = jnp.eye(NSA_KV_HEADS, dtype=F32)
    half = CMP_LEN // 2
    w1r = w1.reshape(CMP_LEN, HEAD_DIM, CMP_HIDDEN)

    def expand(w):
        return jnp.einsum('ldn,gh->lgdhn', w, eye).reshape(half * KV_WIDTH, NSA_KV_HEADS * CMP_HIDDEN).astype(BF16)

    def pos_row(pp):
        return jnp.broadcast_to(pp[:, None, :], (half, NSA_KV_HEADS, HEAD_DIM)).reshape(1, half * KV_WIDTH)

    w2p = jnp.pad(w2, ((0, 0), (0, LANES - HEAD_DIM)))
    w2b = jnp.einsum('nd,gh->gnhd', w2p, eye).reshape(NSA_KV_HEADS * CMP_HIDDEN, NSA_KV_HEADS * LANES).astype(BF16)
    b1t = jnp.tile(b1, NSA_KV_HEADS).reshape(1, NSA_KV_HEADS * CMP_HIDDEN)
    return (pos_row(pos[:half]), pos_row(pos[half:]), expand(w1r[:half]), expand(w1r[half:]), b1t, w2b)


def kernel(x, p, positions, ffn1_norm, ffn1_w_gate, ffn1_w_up, ffn1_w_down, mix_norm, w_in, cmp_k_pos, cmp_k_w1, cmp_k_b1, cmp_k_w2, cmp_v_pos, cmp_v_w1, cmp_v_b1, cmp_v_w2, conv_w, conv_b, w_proj_nsa, w_proj_conv, w_out, ffn2_norm, ffn2_w_gate, ffn2_w_up, ffn2_w_down, ple_norm, ple_w_gate, ple_w_proj, final_norm):
    b, s, d = x.shape
    t = b * s
    depth = ffn1_norm.shape[0]
    assert depth == 1 and d == D_MODEL and s % SEL_CHUNK == 0

    half = HEAD_DIM // 2
    inv_freq = ROPE_THETA ** (-jnp.arange(half, dtype=F32) / half)
    invf = jnp.tile(inv_freq, LANES // half).reshape(1, LANES)
    pos = positions.astype(F32).reshape(t, 1)

    n_cmp_rows = s // CMP_STRIDE
    n_blk = s // SEL_LEN
    c_start = jnp.arange(n_cmp_rows) * CMP_STRIDE
    s_start = jnp.arange(n_blk) * SEL_LEN
    overlap_t = (jnp.clip(jnp.minimum(c_start[None, :] + CMP_LEN, s_start[:, None] + SEL_LEN)
                          - jnp.maximum(c_start[None, :], s_start[:, None]), 0, None).astype(F32)
                 / CMP_LEN).astype(BF16)

    row = lambda v: v.reshape(1, -1)
    h = x.reshape(t, d)
    for li in range(depth):
        wi = w_in[li]
        o = 0
        parts = []
        for sz in (NSA_WIDTH, 6 * KV_WIDTH, 3 * NSA_HEADS, 3 * CONV_WIDTH, D_MODEL, D_MODEL):
            parts.append(wi[:, o:o + sz])
            o += sz
        wq, wkv, wgn, wc, wga, wgb = parts
        per_group = 3 * HPG
        wgn = jnp.pad(wgn.reshape(d, NSA_KV_HEADS, per_group),
                      ((0, 0), (0, 0), (0, GATE_PAD - per_group))).reshape(d, NSA_KV_HEADS * GATE_PAD)

        h1, q, kc_r, vc_r, ks, vs, kw, vw, gates = _head_call(
            h, pos, invf, row(ffn1_norm[li]), ffn1_w_gate[li].astype(BF16), ffn1_w_up[li].astype(BF16),
            ffn1_w_down[li].astype(BF16), row(mix_norm[li]), wq.astype(BF16), wkv.astype(BF16),
            wgn.astype(BF16), s)

        chunk = CMP_STRIDE * KV_WIDTH
        kc, vc = _compress_call(
            kc_r.reshape(b, n_cmp_rows, chunk), vc_r.reshape(b, n_cmp_rows, chunk),
            _compress_params(cmp_k_pos[li], cmp_k_w1[li], cmp_k_b1[li], cmp_k_w2[li]),
            _compress_params(cmp_v_pos[li], cmp_v_w1[li], cmp_v_b1[li], cmp_v_w2[li]))

        seq3 = lambda a: a.reshape(b, s, a.shape[-1])
        o_nsa = _attn_call(seq3(q), seq3(gates), kc, vc, seq3(ks), seq3(vs), seq3(kw), seq3(vw), overlap_t)

        h = _tail_call(
            h1, o_nsa.reshape(t, NSA_WIDTH), p[li].reshape(t, PLE_DIM), s,
            row(mix_norm[li]), wc.astype(BF16), wga.astype(BF16), wgb.astype(BF16),
            conv_w[li], row(conv_b[li]), w_proj_nsa[li].astype(BF16), w_proj_conv[li].astype(BF16),
            w_out[li].astype(BF16), row(ffn2_norm[li]), ffn2_w_gate[li].astype(BF16),
            ffn2_w_up[li].astype(BF16), ffn2_w_down[li].astype(BF16), row(ple_norm[li]),
            ple_w_gate[li].astype(BF16), ple_w_proj[li].astype(BF16), row(final_norm))
    return h.reshape(b, s, d)
```

```python
import functools
import math

import jax
import jax.numpy as jnp
from jax import lax
from jax.experimental import pallas as pl
from jax.experimental.pallas import tpu as pltpu

D_MODEL = 1024
D_FF = 2816
PLE_DIM = 256
EPS = 1e-6
ROPE_THETA = 10000.0

NSA_HEADS = 8
NSA_KV_HEADS = 2
HEAD_DIM = 64
HPG = NSA_HEADS // NSA_KV_HEADS
NSA_WIDTH = NSA_HEADS * HEAD_DIM
KV_WIDTH = NSA_KV_HEADS * HEAD_DIM
CMP_LEN = 32
CMP_STRIDE = 16
CMP_HIDDEN = 256
SEL_LEN = 64
N_SEL = 16
N_LOCAL = 2
WINDOW = 512
Q_BLOCK = 128
FORCED_SCORE = 1e4
INVALID_SCORE = -1e4
CONV_WIDTH = 512
CONV_K = 3

SEL_SHIFT = SEL_LEN.bit_length() - 1

LANES = 128
SUBLANES = 8
VMEM_LIMIT = 58 * 1024 * 1024

HEAD_TM = 1024
TAIL_TM = 512
MXU_DIM = 256
FF_CHUNKS = (6 * MXU_DIM, 5 * MXU_DIM)
SEL_CHUNK = 512
SEL_SPAN_CHUNKS = 2
SEL_UNROLL = 2
GATE_PAD = LANES
BIAS_LANES = LANES - HEAD_DIM
ROPE_PACK = LANES // (HEAD_DIM // 2)
NEG = -(2.0 ** 100)
Q_SCALE = HEAD_DIM ** -0.5 * math.log2(math.e)

BF16 = jnp.bfloat16
F32 = jnp.float32


def _rms(x, g):
    return x * lax.rsqrt(jnp.mean(x * x, axis=-1, keepdims=True) + EPS) * g


def _dot(a, b):
    return jnp.dot(a, b, preferred_element_type=F32)


def _dot_nt(a, b):
    return lax.dot_general(a, b, (((1,), (1,)), ((), ())), preferred_element_type=F32)


def _swiglu_half(xn, wg_ref, wu_ref, wd_ref):
    acc = jnp.zeros((xn.shape[0], D_MODEL), F32)
    assert sum(FF_CHUNKS) == D_FF
    lo = 0
    for width in FF_CHUNKS:
        sl = slice(lo, lo + width)
        lo += width
        gate = _dot(xn, wg_ref[:, sl])
        up = _dot(xn, wu_ref[:, sl])
        act = (gate * jax.nn.sigmoid(gate) * up).astype(BF16)
        acc = acc + _dot(act, wd_ref[sl, :])
    return acc


def _head_kernel(x_ref, pos_ref, invf_ref, n1_ref, wg_ref, wu_ref, wd_ref, n2_ref,
                 wq_ref, wkv_ref, wgn_ref,
                 h1_ref, q_ref, kc_ref, vc_ref, ks_ref, vs_ref, kw_ref, vw_ref, gate_ref, *, seq):
    x = x_ref[...]
    tm = x.shape[0]
    xn = _rms(x, n1_ref[...]).astype(BF16)

    def low_half(width):
        lane = lax.broadcasted_iota(jnp.int32, (tm, width), 1)
        return (lane & (HEAD_DIM - 1)) < (HEAD_DIM // 2)

    h1 = x + 0.5 * _swiglu_half(xn, wg_ref, wu_ref, wd_ref)
    h1_ref[...] = h1
    u = _rms(h1, n2_ref[...]).astype(BF16)

    half = HEAD_DIM // 2
    quarter = tm // ROPE_PACK
    qlane = lax.broadcasted_iota(jnp.int32, (quarter, LANES), 1)
    pos = pos_ref[...]
    packed_pos = jnp.broadcast_to(pos[:quarter], (quarter, LANES))
    for p in range(1, ROPE_PACK):
        packed_pos = jnp.where(qlane >= p * half, pos[p * quarter:(p + 1) * quarter], packed_pos)
    ang = packed_pos * invf_ref[...]

    def unpack(t):
        out = []
        for p in range(ROPE_PACK):
            v = t if p == 0 else pltpu.roll(t, LANES - p * half, 1)
            span = half
            while span < LANES:
                v = jnp.where(qlane < span, v, pltpu.roll(v, span, 1))
                span *= 2
            out.append(v)
        return jnp.concatenate(out, axis=0)

    cos = unpack(jnp.cos(ang))
    sin = unpack(jnp.sin(ang))
    sin_signed = jnp.where(low_half(LANES), -sin, sin)

    def rope(z):
        width = z.shape[1]
        reps = width // LANES
        c = jnp.concatenate([cos] * reps, axis=1) if reps > 1 else cos
        s = jnp.concatenate([sin_signed] * reps, axis=1) if reps > 1 else sin_signed
        partner = jnp.where(low_half(width), pltpu.roll(z, width - HEAD_DIM // 2, 1),
                            pltpu.roll(z, HEAD_DIM // 2, 1))
        return z * c + partner * s

    zq = _dot(u, wq_ref[...])
    q_ref[...] = (rope(zq) * Q_SCALE).astype(BF16)

    lane = lax.broadcasted_iota(jnp.int32, (tm, LANES), 1)
    is_dim = lane < HEAD_DIM
    ones_col = jnp.where(lane == HEAD_DIM, 1.0, 0.0)
    seq_pos = (pl.program_id(0) * tm + lax.broadcasted_iota(jnp.int32, (tm, LANES), 0)) & (seq - 1)
    blk_onehot = jnp.where(lane - HEAD_DIM == ((seq_pos >> SEL_SHIFT) & (BIAS_LANES - 1)), 1.0, 0.0)

    def per_group(z, extra):
        g0 = jnp.where(is_dim, z, extra)
        g1 = jnp.where(is_dim, pltpu.roll(z, HEAD_DIM, 1), extra)
        return jnp.concatenate([g0, g1], axis=1).astype(BF16)

    zkv = _dot(u, wkv_ref[...])
    kc_ref[...] = rope(zkv[:, 0 * KV_WIDTH:1 * KV_WIDTH]).astype(BF16)
    vc_ref[...] = zkv[:, 1 * KV_WIDTH:2 * KV_WIDTH].astype(BF16)
    ks_ref[...] = per_group(rope(zkv[:, 2 * KV_WIDTH:3 * KV_WIDTH]), blk_onehot)
    vs_ref[...] = per_group(zkv[:, 3 * KV_WIDTH:4 * KV_WIDTH], ones_col)
    kw_ref[...] = per_group(rope(zkv[:, 4 * KV_WIDTH:5 * KV_WIDTH]), 0.0)
    vw_ref[...] = per_group(zkv[:, 5 * KV_WIDTH:6 * KV_WIDTH], ones_col)
    gate_ref[...] = jax.nn.sigmoid(_dot(u, wgn_ref[...]))


def _const_spec(shape):
    nd = len(shape)
    return pl.BlockSpec(shape, lambda *_: (0,) * nd, pipeline_mode=pl.Buffered(1))


def _head_call(x2, pos, invf, n1, wg, wu, wd, n2, wq, wkv, wgn, seq):
    t = x2.shape[0]
    tm = HEAD_TM
    assert seq % tm == 0 and seq & (seq - 1) == 0

    def row(width):
        return pl.BlockSpec((tm, width), lambda i: (i, 0))

    def out(width, dtype):
        return jax.ShapeDtypeStruct((t, width), dtype)

    wide = NSA_KV_HEADS * LANES
    return pl.pallas_call(
        functools.partial(_head_kernel, seq=seq),
        grid=(t // tm,),
        in_specs=[row(D_MODEL), row(1), _const_spec((1, LANES)), _const_spec((1, D_MODEL)),
                  _const_spec((D_MODEL, D_FF)), _const_spec((D_MODEL, D_FF)), _const_spec((D_FF, D_MODEL)),
                  _const_spec((1, D_MODEL)), _const_spec((D_MODEL, NSA_WIDTH)),
                  _const_spec((D_MODEL, 6 * KV_WIDTH)), _const_spec((D_MODEL, NSA_KV_HEADS * GATE_PAD))],
        out_specs=[row(D_MODEL), row(NSA_WIDTH), row(KV_WIDTH), row(KV_WIDTH)] + [row(wide)] * 4
                  + [row(NSA_KV_HEADS * GATE_PAD)],
        out_shape=[out(D_MODEL, F32), out(NSA_WIDTH, BF16), out(KV_WIDTH, BF16), out(KV_WIDTH, BF16)]
                  + [out(wide, BF16)] * 4 + [out(NSA_KV_HEADS * GATE_PAD, F32)],
        compiler_params=pltpu.CompilerParams(dimension_semantics=("arbitrary",),
                                             vmem_limit_bytes=VMEM_LIMIT),
        name="head",
    )(x2, pos, invf, n1, wg, wu, wd, n2, wq, wkv, wgn)


def _compress_kernel(kx_ref, vx_ref,
                     kpa_ref, kpb_ref, kwa_ref, kwb_ref, kb1_ref, kw2_ref,
                     vpa_ref, vpb_ref, vwa_ref, vwb_ref, vb1_ref, vw2_ref,
                     kc_ref, vc_ref):
    def one(x_ref, pa_ref, pb_ref, wa_ref, wb_ref, b1_ref, w2_ref):
        x = x_ref[0].astype(F32)
        first = _dot((x + pa_ref[...]).astype(BF16), wa_ref[...])
        second = _dot((x + pb_ref[...]).astype(BF16), wb_ref[...])
        n = first.shape[0]
        hidden = first + pltpu.roll(second, n - 1, 0) + b1_ref[...]
        return _dot(jax.nn.gelu(hidden).astype(BF16), w2_ref[...])

    kc_ref[0] = one(kx_ref, kpa_ref, kpb_ref, kwa_ref, kwb_ref, kb1_ref, kw2_ref).astype(BF16)
    vc_ref[0] = one(vx_ref, vpa_ref, vpb_ref, vwa_ref, vwb_ref, vb1_ref, vw2_ref).astype(BF16)


def _compress_call(kx, vx, kparams, vparams):
    b, nrow, width = kx.shape
    hid = NSA_KV_HEADS * CMP_HIDDEN
    wide = NSA_KV_HEADS * LANES
    x_spec = pl.BlockSpec((1, nrow, width), lambda i: (i, 0, 0))
    w_specs = [_const_spec((1, width)), _const_spec((1, width)), _const_spec((width, hid)),
               _const_spec((width, hid)), _const_spec((1, hid)), _const_spec((hid, wide))]
    o_spec = pl.BlockSpec((1, nrow, wide), lambda i: (i, 0, 0))
    o_shape = jax.ShapeDtypeStruct((b, nrow, wide), BF16)
    return pl.pallas_call(
        _compress_kernel,
        grid=(b,),
        in_specs=[x_spec, x_spec] + w_specs + w_specs,
        out_specs=[o_spec, o_spec],
        out_shape=[o_shape, o_shape],
        compiler_params=pltpu.CompilerParams(dimension_semantics=("arbitrary",),
                                             vmem_limit_bytes=VMEM_LIMIT),
        name="compress",
    )(kx, vx, *kparams, *vparams)


def _attn_kernel(q_ref, qn_ref, gate_ref, kc_ref, vc_ref, ks_ref, vs_ref, kw_ref, vw_ref, ovl_ref, o_ref,
                 qa_ref, oc_ref):
    qb = Q_BLOCK
    rows = HPG * qb
    groups = range(NSA_KV_HEADS)
    step = pl.program_id(1)
    slot = step % 2
    n_cmp = kc_ref.shape[1]
    n_blk = ovl_ref.shape[0]

    def tile(g):
        return slice(g * LANES, (g + 1) * LANES)

    lane = lax.broadcasted_iota(jnp.int32, (qb, LANES), 1)
    is_dim = lane < HEAD_DIM

    def positions(blk):
        return blk * qb + lax.broadcasted_iota(jnp.int32, (qb, 1), 0)

    def split_heads(block_ref):
        q_all = block_ref[0].astype(F32)
        q_heads = []
        for g in groups:
            heads = []
            for hh in range(HPG):
                pair = q_all[:, (g * HPG + hh) // 2 * LANES:((g * HPG + hh) // 2 + 1) * LANES]
                heads.append(pair if hh % 2 == 0 else pltpu.roll(pair, HEAD_DIM, 1))
            q_heads.append(heads)
        return q_heads

    def stack_heads(heads):
        return jnp.concatenate([jnp.where(is_dim, qh, 0.0) for qh in heads], axis=0).astype(BF16)

    def mask_heads(mask, s):
        s3 = s.reshape(HPG, qb, s.shape[-1])
        return jnp.where(mask[None], s3, NEG).reshape(rows, s.shape[-1])

    def normalise(acc):
        return acc * (1.0 / acc[:, HEAD_DIM:HEAD_DIM + 1])

    def select(block_ref, blk, dst, fillers=()):
        fillers = list(fillers)
        tq = positions(blk)
        q_heads = split_heads(block_ref)

        c_last = lax.broadcasted_iota(jnp.int32, (qb, n_cmp), 1) * CMP_STRIDE + (CMP_LEN - 1)
        c_mask = c_last <= tq
        s_c = [mask_heads(c_mask, _dot_nt(stack_heads(q_heads[g]), kc_ref[0, :, tile(g)])) for g in groups]
        sees_cmp = jnp.where(tq >= CMP_LEN - 1, 1.0, 0.0)[None]
        p_c = []
        for g in groups:
            e_c = jnp.exp2(s_c[g] - jnp.max(s_c[g], axis=-1, keepdims=True)).reshape(HPG, qb, n_cmp)
            p_c.append(e_c * (sees_cmp / jnp.sum(e_c, axis=-1, keepdims=True)))
            oc_ref[dst, g] = _dot(p_c[g].reshape(rows, n_cmp).astype(BF16), vc_ref[0, :, tile(g)])

        ovl = ovl_ref[...]
        imp = []
        for g in groups:
            p_sum = jnp.sum(p_c[g], axis=0)
            p_hi = p_sum.astype(BF16)
            p_lo = (p_sum - p_hi.astype(F32)).astype(BF16)
            imp.append(_dot_nt(ovl, p_hi) + _dot_nt(ovl, p_lo))
        imp = jnp.concatenate(imp, axis=1)
        nq = NSA_KV_HEADS * qb
        s_idx = lax.broadcasted_iota(jnp.int32, (n_blk, nq), 0)
        tq_row = blk * qb + (lax.broadcasted_iota(jnp.int32, (1, nq), 1) & (qb - 1))
        blk_q = tq_row >> SEL_SHIFT
        valid = s_idx <= blk_q
        forced = (s_idx == 0) | (valid & (s_idx > blk_q - N_LOCAL))
        score = jnp.where(forced, -jnp.inf, jnp.where(valid, imp, INVALID_SCORE))
        chosen = forced
        s_idx_f = s_idx.astype(F32)
        n_rounds = N_SEL - 1 - N_LOCAL
        every = max(n_rounds // len(fillers), 1) if fillers else 0
        for r in range(n_rounds):
            top = jnp.max(score, axis=0, keepdims=True)
            first = jnp.min(jnp.where(score == top, s_idx_f, float(n_blk)), axis=0, keepdims=True)
            hit = s_idx_f == first
            chosen = chosen | hit
            score = jnp.where(hit, -jnp.inf, score)
            if fillers and r % every == 0:
                fillers.pop(0)()
        for filler in fillers:
            filler()
        bias = jnp.where(chosen & valid, 0.0, NEG)

        for g in groups:
            bias_g = bias[:, g * qb:(g + 1) * qb].T
            for w in range(n_blk // BIAS_LANES):
                b_w = bias_g[:, w * BIAS_LANES:(w + 1) * BIAS_LANES]
                b_w = jnp.concatenate([b_w, b_w], axis=1)
                qa_ref[dst, g, w] = jnp.concatenate([jnp.where(is_dim, qh, b_w) for qh in q_heads[g]],
                                                    axis=0).astype(BF16)

    @pl.when(step == 0)
    def _():
        select(q_ref, 0, 0)

    qs = step * qb
    tq = positions(step)
    ck = SEL_CHUNK
    span_keys = SEL_SPAN_CHUNKS * ck
    spans_per_bias = BIAS_LANES * SEL_LEN // span_keys

    def sel_span(t, carry, n_spans, causal, which=groups):
        carry = list(carry)
        for g in which:
            m_prev, acc = carry[g]
            for u in range(n_spans):
                qa = qa_ref[slot, g, (n_spans * t + u) // spans_per_bias]
                scores = []
                for c in range(SEL_SPAN_CHUNKS):
                    start = pl.multiple_of((n_spans * t + u) * span_keys + c * ck, ck)
                    s = _dot_nt(qa, ks_ref[0, pl.ds(start, ck), tile(g)])
                    if causal:
                        s = mask_heads(start + lax.broadcasted_iota(jnp.int32, (qb, ck), 1) <= tq, s)
                    scores.append((s, start))
                for s, start in scores:
                    m_new = jnp.maximum(m_prev, jnp.max(s, axis=-1, keepdims=True))
                    pv = _dot(jnp.exp2(s - m_new).astype(BF16), vs_ref[0, pl.ds(start, ck), tile(g)])
                    m_prev, acc = m_new, jnp.exp2(m_prev - m_new) * acc + pv
            carry[g] = (m_prev, acc)
        return tuple(carry)

    n_plain = (qs + qb - 1) // span_keys
    carry = ((jnp.full((rows, 1), NEG, F32), jnp.zeros((rows, LANES), F32)),) * NSA_KV_HEADS
    n_wide = n_plain // SEL_UNROLL
    if SEL_UNROLL > 1:
        carry = lax.fori_loop(0, n_wide, functools.partial(sel_span, n_spans=SEL_UNROLL, causal=False), carry)
    carry = lax.fori_loop(n_wide * SEL_UNROLL, n_plain, functools.partial(sel_span, n_spans=1, causal=False),
                          carry)

    state = {"carry": carry, "o_w": [None] * NSA_KV_HEADS}
    span = WINDOW + qb
    w_start = pl.multiple_of(jnp.maximum(qs - WINDOW, 0), qb)
    q_heads = split_heads(q_ref)

    def sweep_end(g):
        def run():
            state["carry"] = sel_span(n_plain, state["carry"], 1, True, which=(g,))
        return run

    def window(g):
        def run():
            kpos_w = w_start + lax.broadcasted_iota(jnp.int32, (qb, span), 1)
            w_mask = (kpos_w <= tq) & (kpos_w > tq - WINDOW)
            s_w = mask_heads(w_mask, _dot_nt(stack_heads(q_heads[g]), kw_ref[0, pl.ds(w_start, span), tile(g)]))
            e_w = jnp.exp2(s_w - jnp.max(s_w, axis=-1, keepdims=True)).astype(BF16)
            state["o_w"][g] = normalise(_dot(e_w, vw_ref[0, pl.ds(w_start, span), tile(g)]))
        return run

    select(qn_ref, step + 1, 1 - slot,
           fillers=[window(g) for g in groups] + [sweep_end(g) for g in groups])
    o_s = [normalise(state["carry"][g][1]) for g in groups]
    o_w = state["o_w"]

    for g in groups:
        gates = gate_ref[0, :, tile(g)]
        o_c = oc_ref[slot, g]
        mixed = []
        for hh in range(HPG):
            sl = slice(hh * qb, (hh + 1) * qb)
            g_c = gates[:, 3 * hh + 0:3 * hh + 1]
            g_s = gates[:, 3 * hh + 1:3 * hh + 2]
            g_w = gates[:, 3 * hh + 2:3 * hh + 3]
            mixed.append(g_c * o_c[sl] + g_s * o_s[g][sl] + g_w * o_w[g][sl])
        for pr in range(HPG // 2):
            odd = pltpu.roll(mixed[2 * pr + 1], HEAD_DIM, 1)
            col = (g * HPG // 2 + pr) * LANES
            o_ref[0, :, col:col + LANES] = jnp.where(is_dim, mixed[2 * pr], odd).astype(BF16)


def _attn_call(q, gates, kc, vc, ks, vs, kw, vw, overlap_t):
    b, s, _ = q.shape
    n_cmp = kc.shape[1]
    n_blk = s // SEL_LEN
    span_keys = SEL_SPAN_CHUNKS * SEL_CHUNK
    assert n_blk % BIAS_LANES == 0 and BIAS_LANES * SEL_LEN % span_keys == 0 and s % span_keys == 0
    wide = NSA_KV_HEADS * LANES
    seq_spec = pl.BlockSpec((1, s, wide), lambda bi, i: (bi, 0, 0), pipeline_mode=pl.Buffered(1))
    cmp_spec = pl.BlockSpec((1, n_cmp, wide), lambda bi, i: (bi, 0, 0), pipeline_mode=pl.Buffered(1))
    n_steps = s // Q_BLOCK
    rows = HPG * Q_BLOCK
    return pl.pallas_call(
        _attn_kernel,
        grid=(b, n_steps),
        in_specs=[pl.BlockSpec((1, Q_BLOCK, NSA_WIDTH), lambda bi, i: (bi, i, 0)),
                  pl.BlockSpec((1, Q_BLOCK, NSA_WIDTH), lambda bi, i: (bi, jnp.minimum(i + 1, n_steps - 1), 0)),
                  pl.BlockSpec((1, Q_BLOCK, NSA_KV_HEADS * GATE_PAD), lambda bi, i: (bi, i, 0)),
                  cmp_spec, cmp_spec, seq_spec, seq_spec, seq_spec, seq_spec,
                  _const_spec(overlap_t.shape)],
        out_specs=pl.BlockSpec((1, Q_BLOCK, NSA_WIDTH), lambda bi, i: (bi, i, 0)),
        out_shape=jax.ShapeDtypeStruct((b, s, NSA_WIDTH), BF16),
        scratch_shapes=[pltpu.VMEM((2, NSA_KV_HEADS, n_blk // BIAS_LANES, rows, LANES), BF16),
                        pltpu.VMEM((2, NSA_KV_HEADS, rows, LANES), F32)],
        compiler_params=pltpu.CompilerParams(dimension_semantics=("arbitrary",) * 2,
                                             vmem_limit_bytes=VMEM_LIMIT),
        name="attention",
    )(q, q, gates, kc, vc, ks, vs, kw, vw, overlap_t)


def _tail_kernel(h1_ref, on_ref, p_ref, nm_ref, wc_ref, wga_ref, wgb_ref, cw_ref, cb_ref,
                 wa_ref, wb_ref, wo_ref, n2_ref, wg_ref, wu_ref, wd_ref,
                 np_ref, pg_ref, pp_ref, nf_ref, out_ref, carry_ref, *, tiles_per_seq):
    i = pl.program_id(0)
    h1 = h1_ref[...]
    tm = h1.shape[0]
    u = _rms(h1, nm_ref[...]).astype(BF16)

    zc = _dot(u, wc_ref[...])
    gate_b = zc[:, :CONV_WIDTH]
    v = zc[:, CONV_WIDTH:2 * CONV_WIDTH] * zc[:, 2 * CONV_WIDTH:]

    @pl.when(i % tiles_per_seq == 0)
    def _():
        carry_ref[...] = jnp.zeros_like(carry_ref)

    prev = carry_ref[...]
    row = lax.broadcasted_iota(jnp.int32, (tm, CONV_WIDTH), 0)
    v1 = jnp.where(row == 0, prev[SUBLANES - 1:SUBLANES], pltpu.roll(v, 1, 0))
    v2 = jnp.where(row == 0, prev[SUBLANES - 2:SUBLANES - 1],
                   jnp.where(row == 1, prev[SUBLANES - 1:SUBLANES], pltpu.roll(v, 2, 0)))
    carry_ref[...] = v[tm - SUBLANES:]
    cw = cw_ref[...]
    y = cw[0:1] * v2 + cw[1:2] * v1 + cw[2:3] * v
    o_conv = (gate_b * (y + cb_ref[...])).astype(BF16)

    merged = (jax.nn.sigmoid(_dot(u, wga_ref[...])) * _dot(on_ref[...], wa_ref[...])
              + jax.nn.sigmoid(_dot(u, wgb_ref[...])) * _dot(o_conv, wb_ref[...]))
    h2 = h1 + _dot(merged.astype(BF16), wo_ref[...])

    h3 = h2 + 0.5 * _swiglu_half(_rms(h2, n2_ref[...]).astype(BF16), wg_ref, wu_ref, wd_ref)

    gate_p = jax.nn.sigmoid(_dot(_rms(h3, np_ref[...]).astype(BF16), pg_ref[...]))
    h4 = h3 + gate_p * _dot(p_ref[...].astype(BF16), pp_ref[...])
    out_ref[...] = _rms(h4, nf_ref[...])


def _tail_call(h1, o_nsa, p2, seq, nm, wc, wga, wgb, cw, cb, wa, wb, wo, n2, wg, wu, wd, npl, pg, pp, nf):
    t = h1.shape[0]
    tm = TAIL_TM
    assert CONV_K - 1 <= SUBLANES and seq % tm == 0

    def row(width):
        return pl.BlockSpec((tm, width), lambda i: (i, 0))

    consts = [nm, wc, wga, wgb, cw, cb, wa, wb, wo, n2, wg, wu, wd, npl, pg, pp, nf]
    return pl.pallas_call(
        functools.partial(_tail_kernel, tiles_per_seq=seq // tm),
        grid=(t // tm,),
        in_specs=[row(D_MODEL), row(NSA_WIDTH), row(PLE_DIM)] + [_const_spec(c.shape) for c in consts],
        out_specs=row(D_MODEL),
        out_shape=jax.ShapeDtypeStruct((t, D_MODEL), F32),
        scratch_shapes=[pltpu.VMEM((SUBLANES, CONV_WIDTH), F32)],
        compiler_params=pltpu.CompilerParams(dimension_semantics=("arbitrary",),
                                             vmem_limit_bytes=VMEM_LIMIT),
        name="tail",
    )(h1, o_nsa, p2, *consts)


def _compress_params(pos, w1, b1, w2):
    eye = jnp.eye(NSA_KV_HEADS, dtype=F32)
    half = CMP_LEN // 2
    w1r = w1.reshape(CMP_LEN, HEAD_DIM, CMP_HIDDEN)

    def expand(w):
        return jnp.einsum('ldn,gh->lgdhn', w, eye).reshape(half * KV_WIDTH, NSA_KV_HEADS * CMP_HIDDEN).astype(BF16)

    def pos_row(pp):
        return jnp.broadcast_to(pp[:, None, :], (half, NSA_KV_HEADS, HEAD_DIM)).reshape(1, half * KV_WIDTH)

    w2p = jnp.pad(w2, ((0, 0), (0, LANES - HEAD_DIM)))
    w2b = jnp.einsum('nd,gh->gnhd', w2p, eye).reshape(NSA_KV_HEADS * CMP_HIDDEN, NSA_KV_HEADS * LANES).astype(BF16)
    b1t = jnp.tile(b1, NSA_KV_HEADS).reshape(1, NSA_KV_HEADS * CMP_HIDDEN)
    return (pos_row(pos[:half]), pos_row(pos[half:]), expand(w1r[:half]), expand(w1r[half:]), b1t, w2b)


def kernel(x, p, positions, ffn1_norm, ffn1_w_gate, ffn1_w_up, ffn1_w_down, mix_norm, w_in, cmp_k_pos, cmp_k_w1, cmp_k_b1, cmp_k_w2, cmp_v_pos, cmp_v_w1, cmp_v_b1, cmp_v_w2, conv_w, conv_b, w_proj_nsa, w_proj_conv, w_out, ffn2_norm, ffn2_w_gate, ffn2_w_up, ffn2_w_down, ple_norm, ple_w_gate, ple_w_proj, final_norm):
    b, s, d = x.shape
    t = b * s
    depth = ffn1_norm.shape[0]
    assert depth == 1 and d == D_MODEL and s % SEL_CHUNK == 0

    half = HEAD_DIM // 2
    inv_freq = ROPE_THETA ** (-jnp.arange(half, dtype=F32) / half)
    invf = jnp.tile(inv_freq, LANES // half).reshape(1, LANES)
    pos = positions.astype(F32).reshape(t, 1)

    n_cmp_rows = s // CMP_STRIDE
    n_blk = s // SEL_LEN
    c_start = jnp.arange(n_cmp_rows) * CMP_STRIDE
    s_start = jnp.arange(n_blk) * SEL_LEN
    overlap_t = (jnp.clip(jnp.minimum(c_start[None, :] + CMP_LEN, s_start[:, None] + SEL_LEN)
                          - jnp.maximum(c_start[None, :], s_start[:, None]), 0, None).astype(F32)
                 / CMP_LEN).astype(BF16)

    row = lambda v: v.reshape(1, -1)
    h = x.reshape(t, d)
    for li in range(depth):
        wi = w_in[li]
        o = 0
        parts = []
        for sz in (NSA_WIDTH, 6 * KV_WIDTH, 3 * NSA_HEADS, 3 * CONV_WIDTH, D_MODEL, D_MODEL):
            parts.append(wi[:, o:o + sz])
            o += sz
        wq, wkv, wgn, wc, wga, wgb = parts
        per_group = 3 * HPG
        wgn = jnp.pad(wgn.reshape(d, NSA_KV_HEADS, per_group),
                      ((0, 0), (0, 0), (0, GATE_PAD - per_group))).reshape(d, NSA_KV_HEADS * GATE_PAD)

        h1, q, kc_r, vc_r, ks, vs, kw, vw, gates = _head_call(
            h, pos, invf, row(ffn1_norm[li]), ffn1_w_gate[li].astype(BF16), ffn1_w_up[li].astype(BF16),
            ffn1_w_down[li].astype(BF16), row(mix_norm[li]), wq.astype(BF16), wkv.astype(BF16),
            wgn.astype(BF16), s)

        chunk = CMP_STRIDE * KV_WIDTH
        kc, vc = _compress_call(
            kc_r.reshape(b, n_cmp_rows, chunk), vc_r.reshape(b, n_cmp_rows, chunk),
            _compress_params(cmp_k_pos[li], cmp_k_w1[li], cmp_k_b1[li], cmp_k_w2[li]),
            _compress_params(cmp_v_pos[li], cmp_v_w1[li], cmp_v_b1[li], cmp_v_w2[li]))

        seq3 = lambda a: a.reshape(b, s, a.shape[-1])
        o_nsa = _attn_call(seq3(q), seq3(gates), kc, vc, seq3(ks), seq3(vs), seq3(kw), seq3(vw), overlap_t)

        h = _tail_call(
            h1, o_nsa.reshape(t, NSA_WIDTH), p[li].reshape(t, PLE_DIM), s,
            row(mix_norm[li]), wc.astype(BF16), wga.astype(BF16), wgb.astype(BF16),
            conv_w[li], row(conv_b[li]), w_proj_nsa[li].astype(BF16), w_proj_conv[li].astype(BF16),
            w_out[li].astype(BF16), row(ffn2_norm[li]), ffn2_w_gate[li].astype(BF16),
            ffn2_w_up[li].astype(BF16), ffn2_w_down[li].astype(BF16), row(ple_norm[li]),
            ple_w_gate[li].astype(BF16), ple_w_proj[li].astype(BF16), row(final_norm))
    return h.reshape(b, s, d)
```

```python
import functools
import math

import jax
import jax.numpy as jnp
from jax import lax
from jax.experimental import pallas as pl
from jax.experimental.pallas import tpu as pltpu

D_MODEL = 1024
D_FF = 2816
PLE_DIM = 256
EPS = 1e-6
ROPE_THETA = 10000.0

NSA_HEADS = 8
NSA_KV_HEADS = 2
HEAD_DIM = 64
HPG = NSA_HEADS // NSA_KV_HEADS
NSA_WIDTH = NSA_HEADS * HEAD_DIM
KV_WIDTH = NSA_KV_HEADS * HEAD_DIM
CMP_LEN = 32
CMP_STRIDE = 16
CMP_HIDDEN = 256
SEL_LEN = 64
N_SEL = 16
N_LOCAL = 2
WINDOW = 512
Q_BLOCK = 128
FORCED_SCORE = 1e4
INVALID_SCORE = -1e4
CONV_WIDTH = 512
CONV_K = 3

SEL_SHIFT = SEL_LEN.bit_length() - 1

LANES = 128
SUBLANES = 8
VMEM_LIMIT = 58 * 1024 * 1024

HEAD_TM = 1024
TAIL_TM = 512
MXU_DIM = 256
FF_CHUNKS = (6 * MXU_DIM, 5 * MXU_DIM)
SEL_CHUNK = 512
SEL_SPAN_CHUNKS = 2
SEL_UNROLL = 2
GATE_PAD = LANES
BIAS_LANES = LANES - HEAD_DIM
ROPE_PACK = LANES // (HEAD_DIM // 2)
NEG = -(2.0 ** 100)
Q_SCALE = HEAD_DIM ** -0.5 * math.log2(math.e)

BF16 = jnp.bfloat16
F32 = jnp.float32


def _rms(x, g):
    return x * lax.rsqrt(jnp.mean(x * x, axis=-1, keepdims=True) + EPS) * g


def _dot(a, b):
    return jnp.dot(a, b, preferred_element_type=F32)


def _dot_nt(a, b):
    return lax.dot_general(a, b, (((1,), (1,)), ((), ())), preferred_element_type=F32)


def _swiglu_half(xn, wg_ref, wu_ref, wd_ref):
    acc = jnp.zeros((xn.shape[0], D_MODEL), F32)
    assert sum(FF_CHUNKS) == D_FF
    lo = 0
    for width in FF_CHUNKS:
        sl = slice(lo, lo + width)
        lo += width
        gate = _dot(xn, wg_ref[:, sl])
        up = _dot(xn, wu_ref[:, sl])
        act = (gate * jax.nn.sigmoid(gate) * up).astype(BF16)
        acc = acc + _dot(act, wd_ref[sl, :])
    return acc


def _head_kernel(x_ref, pos_ref, invf_ref, n1_ref, wg_ref, wu_ref, wd_ref, n2_ref,
                 wq_ref, wkv_ref, wgn_ref,
                 h1_ref, q_ref, kc_ref, vc_ref, ks_ref, vs_ref, kw_ref, vw_ref, gate_ref, stage_ref, *, seq):
    x = x_ref[...]
    tm = x.shape[0]
    xn = _rms(x, n1_ref[...]).astype(BF16)

    def low_half(width):
        lane = lax.broadcasted_iota(jnp.int32, (tm, width), 1)
        return (lane & (HEAD_DIM - 1)) < (HEAD_DIM // 2)

    h1 = x + 0.5 * _swiglu_half(xn, wg_ref, wu_ref, wd_ref)
    h1_ref[...] = h1
    u = _rms(h1, n2_ref[...]).astype(BF16)

    half = HEAD_DIM // 2
    quarter = tm // ROPE_PACK
    qlane = lax.broadcasted_iota(jnp.int32, (quarter, LANES), 1)
    pos = pos_ref[...]
    packed_pos = jnp.broadcast_to(pos[:quarter], (quarter, LANES))
    for p in range(1, ROPE_PACK):
        packed_pos = jnp.where(qlane >= p * half, pos[p * quarter:(p + 1) * quarter], packed_pos)
    ang = packed_pos * invf_ref[...]

    def unpack(t):
        out = []
        for p in range(ROPE_PACK):
            v = t if p == 0 else pltpu.roll(t, LANES - p * half, 1)
            span = half
            while span < LANES:
                v = jnp.where(qlane < span, v, pltpu.roll(v, span, 1))
                span *= 2
            out.append(v)
        return jnp.concatenate(out, axis=0)

    cos = unpack(jnp.cos(ang))
    sin = unpack(jnp.sin(ang))
    sin_signed = jnp.where(low_half(LANES), -sin, sin)

    def rope(z):
        width = z.shape[1]
        reps = width // LANES
        c = jnp.concatenate([cos] * reps, axis=1) if reps > 1 else cos
        s = jnp.concatenate([sin_signed] * reps, axis=1) if reps > 1 else sin_signed
        partner = jnp.where(low_half(width), pltpu.roll(z, width - HEAD_DIM // 2, 1),
                            pltpu.roll(z, HEAD_DIM // 2, 1))
        return z * c + partner * s

    zq = _dot(u, wq_ref[...])
    q_ref[...] = (rope(zq) * Q_SCALE).astype(BF16)

    lane = lax.broadcasted_iota(jnp.int32, (tm, LANES), 1)
    is_dim = lane < HEAD_DIM
    ones_col = jnp.where(lane == HEAD_DIM, 1.0, 0.0)
    seq_pos = (pl.program_id(0) * tm + lax.broadcasted_iota(jnp.int32, (tm, LANES), 0)) & (seq - 1)
    blk_onehot = jnp.where(lane - HEAD_DIM == ((seq_pos >> SEL_SHIFT) & (BIAS_LANES - 1)), 1.0, 0.0)

    def per_group(z, extra):
        g0 = jnp.where(is_dim, z, extra)
        g1 = jnp.where(is_dim, pltpu.roll(z, HEAD_DIM, 1), extra)
        return jnp.concatenate([g0, g1], axis=1).astype(BF16)

    zkv = _dot(u, wkv_ref[...])
    stage_ref[0] = rope(zkv[:, 0 * KV_WIDTH:1 * KV_WIDTH])
    stage_ref[1] = zkv[:, 1 * KV_WIDTH:2 * KV_WIDTH]
    for which, dst_ref in enumerate((kc_ref, vc_ref)):
        for l in range(CMP_STRIDE):
            rows_l = stage_ref[which, pl.ds(l, tm // CMP_STRIDE, stride=CMP_STRIDE), :]
            dst_ref[:, l * KV_WIDTH:(l + 1) * KV_WIDTH] = rows_l.astype(BF16)
    ks_ref[...] = per_group(rope(zkv[:, 2 * KV_WIDTH:3 * KV_WIDTH]), blk_onehot)
    vs_ref[...] = per_group(zkv[:, 3 * KV_WIDTH:4 * KV_WIDTH], ones_col)
    kw_ref[...] = per_group(rope(zkv[:, 4 * KV_WIDTH:5 * KV_WIDTH]), 0.0)
    vw_ref[...] = per_group(zkv[:, 5 * KV_WIDTH:6 * KV_WIDTH], ones_col)
    gate_ref[...] = jax.nn.sigmoid(_dot(u, wgn_ref[...]))


def _const_spec(shape):
    nd = len(shape)
    return pl.BlockSpec(shape, lambda *_: (0,) * nd, pipeline_mode=pl.Buffered(1))


def _head_call(x2, pos, invf, n1, wg, wu, wd, n2, wq, wkv, wgn, seq):
    t = x2.shape[0]
    tm = HEAD_TM
    assert seq % tm == 0 and seq & (seq - 1) == 0

    def row(width):
        return pl.BlockSpec((tm, width), lambda i: (i, 0))

    def out(width, dtype):
        return jax.ShapeDtypeStruct((t, width), dtype)

    wide = NSA_KV_HEADS * LANES
    chunk = CMP_STRIDE * KV_WIDTH
    chunked = pl.BlockSpec((tm // CMP_STRIDE, chunk), lambda i: (i, 0))
    chunked_shape = jax.ShapeDtypeStruct((t // CMP_STRIDE, chunk), BF16)
    return pl.pallas_call(
        functools.partial(_head_kernel, seq=seq),
        grid=(t // tm,),
        in_specs=[row(D_MODEL), row(1), _const_spec((1, LANES)), _const_spec((1, D_MODEL)),
                  _const_spec((D_MODEL, D_FF)), _const_spec((D_MODEL, D_FF)), _const_spec((D_FF, D_MODEL)),
                  _const_spec((1, D_MODEL)), _const_spec((D_MODEL, NSA_WIDTH)),
                  _const_spec((D_MODEL, 6 * KV_WIDTH)), _const_spec((D_MODEL, NSA_KV_HEADS * GATE_PAD))],
        out_specs=[row(D_MODEL), row(NSA_WIDTH), chunked, chunked] + [row(wide)] * 4
                  + [row(NSA_KV_HEADS * GATE_PAD)],
        out_shape=[out(D_MODEL, F32), out(NSA_WIDTH, BF16), chunked_shape, chunked_shape]
                  + [out(wide, BF16)] * 4 + [out(NSA_KV_HEADS * GATE_PAD, F32)],
        scratch_shapes=[pltpu.VMEM((2, tm, KV_WIDTH), F32)],
        compiler_params=pltpu.CompilerParams(dimension_semantics=("arbitrary",),
                                             vmem_limit_bytes=VMEM_LIMIT),
        name="head",
    )(x2, pos, invf, n1, wg, wu, wd, n2, wq, wkv, wgn)


def _compress_kernel(kx_ref, vx_ref,
                     kpa_ref, kpb_ref, kwa_ref, kwb_ref, kb1_ref, kw2_ref,
                     vpa_ref, vpb_ref, vwa_ref, vwb_ref, vb1_ref, vw2_ref,
                     kc_ref, vc_ref):
    def one(x_ref, pa_ref, pb_ref, wa_ref, wb_ref, b1_ref, w2_ref):
        x = x_ref[0].astype(F32)
        first = _dot((x + pa_ref[...]).astype(BF16), wa_ref[...])
        second = _dot((x + pb_ref[...]).astype(BF16), wb_ref[...])
        n = first.shape[0]
        hidden = first + pltpu.roll(second, n - 1, 0) + b1_ref[...]
        return _dot(jax.nn.gelu(hidden).astype(BF16), w2_ref[...])

    kc_ref[0] = one(kx_ref, kpa_ref, kpb_ref, kwa_ref, kwb_ref, kb1_ref, kw2_ref).astype(BF16)
    vc_ref[0] = one(vx_ref, vpa_ref, vpb_ref, vwa_ref, vwb_ref, vb1_ref, vw2_ref).astype(BF16)


def _compress_call(kx, vx, kparams, vparams):
    b, nrow, width = kx.shape
    hid = NSA_KV_HEADS * CMP_HIDDEN
    wide = NSA_KV_HEADS * LANES
    x_spec = pl.BlockSpec((1, nrow, width), lambda i: (i, 0, 0))
    w_specs = [_const_spec((1, width)), _const_spec((1, width)), _const_spec((width, hid)),
               _const_spec((width, hid)), _const_spec((1, hid)), _const_spec((hid, wide))]
    o_spec = pl.BlockSpec((1, nrow, wide), lambda i: (i, 0, 0))
    o_shape = jax.ShapeDtypeStruct((b, nrow, wide), BF16)
    return pl.pallas_call(
        _compress_kernel,
        grid=(b,),
        in_specs=[x_spec, x_spec] + w_specs + w_specs,
        out_specs=[o_spec, o_spec],
        out_shape=[o_shape, o_shape],
        compiler_params=pltpu.CompilerParams(dimension_semantics=("arbitrary",),
                                             vmem_limit_bytes=VMEM_LIMIT),
        name="compress",
    )(kx, vx, *kparams, *vparams)


def _attn_kernel(q_ref, qn_ref, gate_ref, kc_ref, vc_ref, ks_ref, vs_ref, kw_ref, vw_ref, ovl_ref, o_ref,
                 qa_ref, oc_ref):
    qb = Q_BLOCK
    rows = HPG * qb
    groups = range(NSA_KV_HEADS)
    step = pl.program_id(1)
    slot = step % 2
    n_cmp = kc_ref.shape[1]
    n_blk = ovl_ref.shape[0]

    def tile(g):
        return slice(g * LANES, (g + 1) * LANES)

    lane = lax.broadcasted_iota(jnp.int32, (qb, LANES), 1)
    is_dim = lane < HEAD_DIM

    def positions(blk):
        return blk * qb + lax.broadcasted_iota(jnp.int32, (qb, 1), 0)

    def split_heads(block_ref):
        q_all = block_ref[0].astype(F32)
        q_heads = []
        for g in groups:
            heads = []
            for hh in range(HPG):
                pair = q_all[:, (g * HPG + hh) // 2 * LANES:((g * HPG + hh) // 2 + 1) * LANES]
                heads.append(pair if hh % 2 == 0 else pltpu.roll(pair, HEAD_DIM, 1))
            q_heads.append(heads)
        return q_heads

    def stack_heads(heads):
        return jnp.concatenate([jnp.where(is_dim, qh, 0.0) for qh in heads], axis=0).astype(BF16)

    def mask_heads(mask, s):
        s3 = s.reshape(HPG, qb, s.shape[-1])
        return jnp.where(mask[None], s3, NEG).reshape(rows, s.shape[-1])

    def normalise(acc):
        return acc * (1.0 / acc[:, HEAD_DIM:HEAD_DIM + 1])

    def select(block_ref, blk, dst, fillers=()):
        fillers = list(fillers)
        tq = positions(blk)
        q_heads = split_heads(block_ref)

        c_last = lax.broadcasted_iota(jnp.int32, (qb, n_cmp), 1) * CMP_STRIDE + (CMP_LEN - 1)
        c_mask = c_last <= tq
        s_c = [mask_heads(c_mask, _dot_nt(stack_heads(q_heads[g]), kc_ref[0, :, tile(g)])) for g in groups]
        sees_cmp = jnp.where(tq >= CMP_LEN - 1, 1.0, 0.0)[None]
        p_c = []
        for g in groups:
            e_c = jnp.exp2(s_c[g] - jnp.max(s_c[g], axis=-1, keepdims=True)).reshape(HPG, qb, n_cmp)
            p_c.append(e_c * (sees_cmp / jnp.sum(e_c, axis=-1, keepdims=True)))
            oc_ref[dst, g] = _dot(p_c[g].reshape(rows, n_cmp).astype(BF16), vc_ref[0, :, tile(g)])

        ovl = ovl_ref[...]
        imp = []
        for g in groups:
            p_sum = jnp.sum(p_c[g], axis=0)
            p_hi = p_sum.astype(BF16)
            p_lo = (p_sum - p_hi.astype(F32)).astype(BF16)
            imp.append(_dot_nt(ovl, p_hi) + _dot_nt(ovl, p_lo))
        imp = jnp.concatenate(imp, axis=1)
        nq = NSA_KV_HEADS * qb
        s_idx = lax.broadcasted_iota(jnp.int32, (n_blk, nq), 0)
        tq_row = blk * qb + (lax.broadcasted_iota(jnp.int32, (1, nq), 1) & (qb - 1))
        blk_q = tq_row >> SEL_SHIFT
        valid = s_idx <= blk_q
        forced = (s_idx == 0) | (valid & (s_idx > blk_q - N_LOCAL))
        score = jnp.where(forced, -jnp.inf, jnp.where(valid, imp, INVALID_SCORE))
        chosen = forced
        s_idx_f = s_idx.astype(F32)
        n_rounds = N_SEL - 1 - N_LOCAL
        every = max(n_rounds // len(fillers), 1) if fillers else 0
        for r in range(n_rounds):
            top = jnp.max(score, axis=0, keepdims=True)
            first = jnp.min(jnp.where(score == top, s_idx_f, float(n_blk)), axis=0, keepdims=True)
            hit = s_idx_f == first
            chosen = chosen | hit
            score = jnp.where(hit, -jnp.inf, score)
            if fillers and r % every == 0:
                fillers.pop(0)()
        for filler in fillers:
            filler()
        bias = jnp.where(chosen & valid, 0.0, NEG)

        for g in groups:
            bias_g = bias[:, g * qb:(g + 1) * qb].T
            for w in range(n_blk // BIAS_LANES):
                b_w = bias_g[:, w * BIAS_LANES:(w + 1) * BIAS_LANES]
                b_w = jnp.concatenate([b_w, b_w], axis=1)
                qa_ref[dst, g, w] = jnp.concatenate([jnp.where(is_dim, qh, b_w) for qh in q_heads[g]],
                                                    axis=0).astype(BF16)

    @pl.when(step == 0)
    def _():
        select(q_ref, 0, 0)

    qs = step * qb
    tq = positions(step)
    ck = SEL_CHUNK
    span_keys = SEL_SPAN_CHUNKS * ck
    spans_per_bias = BIAS_LANES * SEL_LEN // span_keys
    own_chunk = (qs // ck) % SEL_SPAN_CHUNKS

    def sel_span(t, carry, n_spans, causal, which=groups):
        carry = list(carry)
        for g in which:
            m_prev, acc = carry[g]
            for u in range(n_spans):
                qa = qa_ref[slot, g, (n_spans * t + u) // spans_per_bias]
                scores = []
                for c in range(SEL_SPAN_CHUNKS):
                    c_eff = (own_chunk + c) % SEL_SPAN_CHUNKS if causal else c
                    start = pl.multiple_of((n_spans * t + u) * span_keys + c_eff * ck, ck)
                    s = _dot_nt(qa, ks_ref[0, pl.ds(start, ck), tile(g)])
                    if causal and c == 0:
                        s = mask_heads(start + lax.broadcasted_iota(jnp.int32, (qb, ck), 1) <= tq, s)
                    scores.append((s, start))
                for s, start in scores:
                    m_new = jnp.maximum(m_prev, jnp.max(s, axis=-1, keepdims=True))
                    pv = _dot(jnp.exp2(s - m_new).astype(BF16), vs_ref[0, pl.ds(start, ck), tile(g)])
                    m_prev, acc = m_new, jnp.exp2(m_prev - m_new) * acc + pv
            carry[g] = (m_prev, acc)
        return tuple(carry)

    n_plain = (qs + qb - 1) // span_keys
    carry = ((jnp.full((rows, 1), NEG, F32), jnp.zeros((rows, LANES), F32)),) * NSA_KV_HEADS
    n_wide = n_plain // SEL_UNROLL
    if SEL_UNROLL > 1:
        carry = lax.fori_loop(0, n_wide, functools.partial(sel_span, n_spans=SEL_UNROLL, causal=False), carry)
    carry = lax.fori_loop(n_wide * SEL_UNROLL, n_plain, functools.partial(sel_span, n_spans=1, causal=False),
                          carry)

    state = {"carry": carry, "o_w": [None] * NSA_KV_HEADS}
    span = WINDOW + qb
    w_start = pl.multiple_of(jnp.maximum(qs - WINDOW, 0), qb)
    q_heads = split_heads(q_ref)

    def sweep_end(g):
        def run():
            state["carry"] = sel_span(n_plain, state["carry"], 1, True, which=(g,))
        return run

    def window(g):
        def run():
            kpos_w = w_start + lax.broadcasted_iota(jnp.int32, (qb, span), 1)
            w_mask = (kpos_w <= tq) & (kpos_w > tq - WINDOW)
            s_w = mask_heads(w_mask, _dot_nt(stack_heads(q_heads[g]), kw_ref[0, pl.ds(w_start, span), tile(g)]))
            e_w = jnp.exp2(s_w - jnp.max(s_w, axis=-1, keepdims=True)).astype(BF16)
            state["o_w"][g] = normalise(_dot(e_w, vw_ref[0, pl.ds(w_start, span), tile(g)]))
        return run

    select(qn_ref, step + 1, 1 - slot,
           fillers=[window(g) for g in groups] + [sweep_end(g) for g in groups])
    o_s = [normalise(state["carry"][g][1]) for g in groups]
    o_w = state["o_w"]

    for g in groups:
        gates = gate_ref[0, :, tile(g)]
        o_c = oc_ref[slot, g]
        mixed = []
        for hh in range(HPG):
            sl = slice(hh * qb, (hh + 1) * qb)
            g_c = gates[:, 3 * hh + 0:3 * hh + 1]
            g_s = gates[:, 3 * hh + 1:3 * hh + 2]
            g_w = gates[:, 3 * hh + 2:3 * hh + 3]
            mixed.append(g_c * o_c[sl] + g_s * o_s[g][sl] + g_w * o_w[g][sl])
        for pr in range(HPG // 2):
            odd = pltpu.roll(mixed[2 * pr + 1], HEAD_DIM, 1)
            col = (g * HPG // 2 + pr) * LANES
            o_ref[0, :, col:col + LANES] = jnp.where(is_dim, mixed[2 * pr], odd).astype(BF16)


def _attn_call(q, gates, kc, vc, ks, vs, kw, vw, overlap_t):
    b, s, _ = q.shape
    n_cmp = kc.shape[1]
    n_blk = s // SEL_LEN
    span_keys = SEL_SPAN_CHUNKS * SEL_CHUNK
    assert n_blk % BIAS_LANES == 0 and BIAS_LANES * SEL_LEN % span_keys == 0 and s % span_keys == 0
    wide = NSA_KV_HEADS * LANES
    seq_spec = pl.BlockSpec((1, s, wide), lambda bi, i: (bi, 0, 0), pipeline_mode=pl.Buffered(1))
    cmp_spec = pl.BlockSpec((1, n_cmp, wide), lambda bi, i: (bi, 0, 0), pipeline_mode=pl.Buffered(1))
    n_steps = s // Q_BLOCK
    rows = HPG * Q_BLOCK
    return pl.pallas_call(
        _attn_kernel,
        grid=(b, n_steps),
        in_specs=[pl.BlockSpec((1, Q_BLOCK, NSA_WIDTH), lambda bi, i: (bi, i, 0)),
                  pl.BlockSpec((1, Q_BLOCK, NSA_WIDTH), lambda bi, i: (bi, jnp.minimum(i + 1, n_steps - 1), 0)),
                  pl.BlockSpec((1, Q_BLOCK, NSA_KV_HEADS * GATE_PAD), lambda bi, i: (bi, i, 0)),
                  cmp_spec, cmp_spec, seq_spec, seq_spec, seq_spec, seq_spec,
                  _const_spec(overlap_t.shape)],
        out_specs=pl.BlockSpec((1, Q_BLOCK, NSA_WIDTH), lambda bi, i: (bi, i, 0)),
        out_shape=jax.ShapeDtypeStruct((b, s, NSA_WIDTH), BF16),
        scratch_shapes=[pltpu.VMEM((2, NSA_KV_HEADS, n_blk // BIAS_LANES, rows, LANES), BF16),
                        pltpu.VMEM((2, NSA_KV_HEADS, rows, LANES), F32)],
        compiler_params=pltpu.CompilerParams(dimension_semantics=("arbitrary",) * 2,
                                             vmem_limit_bytes=VMEM_LIMIT),
        name="attention",
    )(q, q, gates, kc, vc, ks, vs, kw, vw, overlap_t)


def _tail_kernel(h1_ref, on_ref, p_ref, nm_ref, wc_ref, wga_ref, wgb_ref, cw_ref, cb_ref,
                 wa_ref, wb_ref, wo_ref, n2_ref, wg_ref, wu_ref, wd_ref,
                 np_ref, pg_ref, pp_ref, nf_ref, out_ref, carry_ref, *, tiles_per_seq):
    i = pl.program_id(0)
    h1 = h1_ref[...]
    tm = h1.shape[0]
    u = _rms(h1, nm_ref[...]).astype(BF16)

    zc = _dot(u, wc_ref[...])
    gate_b = zc[:, :CONV_WIDTH]
    v = zc[:, CONV_WIDTH:2 * CONV_WIDTH] * zc[:, 2 * CONV_WIDTH:]

    @pl.when(i % tiles_per_seq == 0)
    def _():
        carry_ref[...] = jnp.zeros_like(carry_ref)

    prev = carry_ref[...]
    row = lax.broadcasted_iota(jnp.int32, (tm, CONV_WIDTH), 0)
    v1 = jnp.where(row == 0, prev[SUBLANES - 1:SUBLANES], pltpu.roll(v, 1, 0))
    v2 = jnp.where(row == 0, prev[SUBLANES - 2:SUBLANES - 1],
                   jnp.where(row == 1, prev[SUBLANES - 1:SUBLANES], pltpu.roll(v, 2, 0)))
    carry_ref[...] = v[tm - SUBLANES:]
    cw = cw_ref[...]
    y = cw[0:1] * v2 + cw[1:2] * v1 + cw[2:3] * v
    o_conv = (gate_b * (y + cb_ref[...])).astype(BF16)

    merged = (jax.nn.sigmoid(_dot(u, wga_ref[...])) * _dot(on_ref[...], wa_ref[...])
              + jax.nn.sigmoid(_dot(u, wgb_ref[...])) * _dot(o_conv, wb_ref[...]))
    h2 = h1 + _dot(merged.astype(BF16), wo_ref[...])

    h3 = h2 + 0.5 * _swiglu_half(_rms(h2, n2_ref[...]).astype(BF16), wg_ref, wu_ref, wd_ref)

    gate_p = jax.nn.sigmoid(_dot(_rms(h3, np_ref[...]).astype(BF16), pg_ref[...]))
    h4 = h3 + gate_p * _dot(p_ref[...].astype(BF16), pp_ref[...])
    out_ref[...] = _rms(h4, nf_ref[...])


def _tail_call(h1, o_nsa, p2, seq, nm, wc, wga, wgb, cw, cb, wa, wb, wo, n2, wg, wu, wd, npl, pg, pp, nf):
    t = h1.shape[0]
    tm = TAIL_TM
    assert CONV_K - 1 <= SUBLANES and seq % tm == 0

    def row(width):
        return pl.BlockSpec((tm, width), lambda i: (i, 0))

    consts = [nm, wc, wga, wgb, cw, cb, wa, wb, wo, n2, wg, wu, wd, npl, pg, pp, nf]
    return pl.pallas_call(
        functools.partial(_tail_kernel, tiles_per_seq=seq // tm),
        grid=(t // tm,),
        in_specs=[row(D_MODEL), row(NSA_WIDTH), row(PLE_DIM)] + [_const_spec(c.shape) for c in consts],
        out_specs=row(D_MODEL),
        out_shape=jax.ShapeDtypeStruct((t, D_MODEL), F32),
        scratch_shapes=[pltpu.VMEM((SUBLANES, CONV_WIDTH), F32)],
        compiler_params=pltpu.CompilerParams(dimension_semantics=("arbitrary",),
                                             vmem_limit_bytes=VMEM_LIMIT),
        name="tail",
    )(h1, o_nsa, p2, *consts)


def _compress_params(pos, w1, b1, w2):
    eye = jnp.eye(NSA_KV_HEADS, dtype=F32)
    half = CMP_LEN // 2
    w1r = w1.reshape(CMP_LEN, HEAD_DIM, CMP_HIDDEN)

    def expand(w):
        return jnp.einsum('ldn,gh->lgdhn', w, eye).reshape(half * KV_WIDTH, NSA_KV_HEADS * CMP_HIDDEN).astype(BF16)

    def pos_row(pp):
        return jnp.broadcast_to(pp[:, None, :], (half, NSA_KV_HEADS, HEAD_DIM)).reshape(1, half * KV_WIDTH)

    w2p = jnp.pad(w2, ((0, 0), (0, LANES - HEAD_DIM)))
    w2b = jnp.einsum('nd,gh->gnhd', w2p, eye).reshape(NSA_KV_HEADS * CMP_HIDDEN, NSA_KV_HEADS * LANES).astype(BF16)
    b1t = jnp.tile(b1, NSA_KV_HEADS).reshape(1, NSA_KV_HEADS * CMP_HIDDEN)
    return (pos_row(pos[:half]), pos_row(pos[half:]), expand(w1r[:half]), expand(w1r[half:]), b1t, w2b)


def kernel(x, p, positions, ffn1_norm, ffn1_w_gate, ffn1_w_up, ffn1_w_down, mix_norm, w_in, cmp_k_pos, cmp_k_w1, cmp_k_b1, cmp_k_w2, cmp_v_pos, cmp_v_w1, cmp_v_b1, cmp_v_w2, conv_w, conv_b, w_proj_nsa, w_proj_conv, w_out, ffn2_norm, ffn2_w_gate, ffn2_w_up, ffn2_w_down, ple_norm, ple_w_gate, ple_w_proj, final_norm):
    b, s, d = x.shape
    t = b * s
    depth = ffn1_norm.shape[0]
    assert depth == 1 and d == D_MODEL and s % SEL_CHUNK == 0

    half = HEAD_DIM // 2
    inv_freq = ROPE_THETA ** (-jnp.arange(half, dtype=F32) / half)
    invf = jnp.tile(inv_freq, LANES // half).reshape(1, LANES)
    pos = positions.astype(F32).reshape(t, 1)

    n_cmp_rows = s // CMP_STRIDE
    n_blk = s // SEL_LEN
    c_start = jnp.arange(n_cmp_rows) * CMP_STRIDE
    s_start = jnp.arange(n_blk) * SEL_LEN
    overlap_t = (jnp.clip(jnp.minimum(c_start[None, :] + CMP_LEN, s_start[:, None] + SEL_LEN)
                          - jnp.maximum(c_start[None, :], s_start[:, None]), 0, None).astype(F32)
                 / CMP_LEN).astype(BF16)

    row = lambda v: v.reshape(1, -1)
    h = x.reshape(t, d)
    for li in range(depth):
        wi = w_in[li]
        o = 0
        parts = []
        for sz in (NSA_WIDTH, 6 * KV_WIDTH, 3 * NSA_HEADS, 3 * CONV_WIDTH, D_MODEL, D_MODEL):
            parts.append(wi[:, o:o + sz])
            o += sz
        wq, wkv, wgn, wc, wga, wgb = parts
        per_group = 3 * HPG
        wgn = jnp.pad(wgn.reshape(d, NSA_KV_HEADS, per_group),
                      ((0, 0), (0, 0), (0, GATE_PAD - per_group))).reshape(d, NSA_KV_HEADS * GATE_PAD)

        h1, q, kc_r, vc_r, ks, vs, kw, vw, gates = _head_call(
            h, pos, invf, row(ffn1_norm[li]), ffn1_w_gate[li].astype(BF16), ffn1_w_up[li].astype(BF16),
            ffn1_w_down[li].astype(BF16), row(mix_norm[li]), wq.astype(BF16), wkv.astype(BF16),
            wgn.astype(BF16), s)

        chunk = CMP_STRIDE * KV_WIDTH
        kc, vc = _compress_call(
            kc_r.reshape(b, n_cmp_rows, chunk), vc_r.reshape(b, n_cmp_rows, chunk),
            _compress_params(cmp_k_pos[li], cmp_k_w1[li], cmp_k_b1[li], cmp_k_w2[li]),
            _compress_params(cmp_v_pos[li], cmp_v_w1[li], cmp_v_b1[li], cmp_v_w2[li]))

        seq3 = lambda a: a.reshape(b, s, a.shape[-1])
        o_nsa = _attn_call(seq3(q), seq3(gates), kc, vc, seq3(ks), seq3(vs), seq3(kw), seq3(vw), overlap_t)

        h = _tail_call(
            h1, o_nsa.reshape(t, NSA_WIDTH), p[li].reshape(t, PLE_DIM), s,
            row(mix_norm[li]), wc.astype(BF16), wga.astype(BF16), wgb.astype(BF16),
            conv_w[li], row(conv_b[li]), w_proj_nsa[li].astype(BF16), w_proj_conv[li].astype(BF16),
            w_out[li].astype(BF16), row(ffn2_norm[li]), ffn2_w_gate[li].astype(BF16),
            ffn2_w_up[li].astype(BF16), ffn2_w_down[li].astype(BF16), row(ple_norm[li]),
            ple_w_gate[li].astype(BF16), ple_w_proj[li].astype(BF16), row(final_norm))
    return h.reshape(b, s, d)
```

```python
import functools
import math

import jax
import jax.numpy as jnp
from jax import lax
from jax.experimental import pallas as pl
from jax.experimental.pallas import tpu as pltpu

D_MODEL = 1024
D_FF = 2816
PLE_DIM = 256
EPS = 1e-6
ROPE_THETA = 10000.0

NSA_HEADS = 8
NSA_KV_HEADS = 2
HEAD_DIM = 64
HPG = NSA_HEADS // NSA_KV_HEADS
NSA_WIDTH = NSA_HEADS * HEAD_DIM
KV_WIDTH = NSA_KV_HEADS * HEAD_DIM
CMP_LEN = 32
CMP_STRIDE = 16
CMP_HIDDEN = 256
SEL_LEN = 64
N_SEL = 16
N_LOCAL = 2
WINDOW = 512
Q_BLOCK = 128
FORCED_SCORE = 1e4
INVALID_SCORE = -1e4
CONV_WIDTH = 512
CONV_K = 3

SEL_SHIFT = SEL_LEN.bit_length() - 1

LANES = 128
SUBLANES = 8
VMEM_LIMIT = 58 * 1024 * 1024

HEAD_TM = 1024
TAIL_TM = 512
MXU_DIM = 256
FF_CHUNKS = (6 * MXU_DIM, 5 * MXU_DIM)
SEL_CHUNK = 512
SEL_SPAN_CHUNKS = 2
SEL_UNROLL = 2
GATE_PAD = LANES
BIAS_LANES = LANES - HEAD_DIM
ROPE_PACK = LANES // (HEAD_DIM // 2)
NEG = -(2.0 ** 100)
Q_SCALE = HEAD_DIM ** -0.5 * math.log2(math.e)

BF16 = jnp.bfloat16
F32 = jnp.float32


def _rms(x, g):
    return x * lax.rsqrt(jnp.mean(x * x, axis=-1, keepdims=True) + EPS) * g


def _dot(a, b):
    return jnp.dot(a, b, preferred_element_type=F32)


def _dot_nt(a, b):
    return lax.dot_general(a, b, (((1,), (1,)), ((), ())), preferred_element_type=F32)


def _swiglu_half(xn, wg_ref, wu_ref, wd_ref):
    acc = jnp.zeros((xn.shape[0], D_MODEL), F32)
    assert sum(FF_CHUNKS) == D_FF
    lo = 0
    for width in FF_CHUNKS:
        sl = slice(lo, lo + width)
        lo += width
        gate = _dot(xn, wg_ref[:, sl])
        up = _dot(xn, wu_ref[:, sl])
        act = (gate * jax.nn.sigmoid(gate) * up).astype(BF16)
        acc = acc + _dot(act, wd_ref[sl, :])
    return acc


def _head_kernel(x_ref, pos_ref, invf_ref, n1_ref, wg_ref, wu_ref, wd_ref, n2_ref,
                 wq_ref, wkv_ref, wgn_ref,
                 h1_ref, q_ref, kc_ref, vc_ref, ks_ref, vs_ref, kw_ref, vw_ref, gate_ref, stage_ref, *, seq):
    x = x_ref[...]
    tm = x.shape[0]
    xn = _rms(x, n1_ref[...]).astype(BF16)

    def low_half(width):
        lane = lax.broadcasted_iota(jnp.int32, (tm, width), 1)
        return (lane & (HEAD_DIM - 1)) < (HEAD_DIM // 2)

    h1 = x + 0.5 * _swiglu_half(xn, wg_ref, wu_ref, wd_ref)
    h1_ref[...] = h1
    u = _rms(h1, n2_ref[...]).astype(BF16)

    half = HEAD_DIM // 2
    quarter = tm // ROPE_PACK
    qlane = lax.broadcasted_iota(jnp.int32, (quarter, LANES), 1)
    pos = pos_ref[...]
    packed_pos = jnp.broadcast_to(pos[:quarter], (quarter, LANES))
    for p in range(1, ROPE_PACK):
        packed_pos = jnp.where(qlane >= p * half, pos[p * quarter:(p + 1) * quarter], packed_pos)
    ang = packed_pos * invf_ref[...]

    def unpack(t):
        out = []
        for p in range(ROPE_PACK):
            v = t if p == 0 else pltpu.roll(t, LANES - p * half, 1)
            span = half
            while span < LANES:
                v = jnp.where(qlane < span, v, pltpu.roll(v, span, 1))
                span *= 2
            out.append(v)
        return jnp.concatenate(out, axis=0)

    cos = unpack(jnp.cos(ang))
    sin = unpack(jnp.sin(ang))
    sin_signed = jnp.where(low_half(LANES), -sin, sin)

    def rope(z):
        width = z.shape[1]
        reps = width // LANES
        c = jnp.concatenate([cos] * reps, axis=1) if reps > 1 else cos
        s = jnp.concatenate([sin_signed] * reps, axis=1) if reps > 1 else sin_signed
        partner = jnp.where(low_half(width), pltpu.roll(z, width - HEAD_DIM // 2, 1),
                            pltpu.roll(z, HEAD_DIM // 2, 1))
        return z * c + partner * s

    zq = _dot(u, wq_ref[...])
    q_ref[...] = (rope(zq) * Q_SCALE).astype(BF16)

    lane = lax.broadcasted_iota(jnp.int32, (tm, LANES), 1)
    is_dim = lane < HEAD_DIM
    ones_col = jnp.where(lane == HEAD_DIM, 1.0, 0.0)
    seq_pos = (pl.program_id(0) * tm + lax.broadcasted_iota(jnp.int32, (tm, LANES), 0)) & (seq - 1)
    blk_onehot = jnp.where(lane - HEAD_DIM == ((seq_pos >> SEL_SHIFT) & (BIAS_LANES - 1)), 1.0, 0.0)

    def per_group(z, extra):
        g0 = jnp.where(is_dim, z, extra)
        g1 = jnp.where(is_dim, pltpu.roll(z, HEAD_DIM, 1), extra)
        return jnp.concatenate([g0, g1], axis=1).astype(BF16)

    zkv = _dot(u, wkv_ref[...])
    stage_ref[0] = rope(zkv[:, 0 * KV_WIDTH:1 * KV_WIDTH])
    stage_ref[1] = zkv[:, 1 * KV_WIDTH:2 * KV_WIDTH]
    for which, dst_ref in enumerate((kc_ref, vc_ref)):
        for l in range(CMP_STRIDE):
            rows_l = stage_ref[which, pl.ds(l, tm // CMP_STRIDE, stride=CMP_STRIDE), :]
            dst_ref[:, l * KV_WIDTH:(l + 1) * KV_WIDTH] = rows_l.astype(BF16)
    ks_ref[...] = per_group(rope(zkv[:, 2 * KV_WIDTH:3 * KV_WIDTH]), blk_onehot)
    vs_ref[...] = per_group(zkv[:, 3 * KV_WIDTH:4 * KV_WIDTH], ones_col)
    kw_ref[...] = per_group(rope(zkv[:, 4 * KV_WIDTH:5 * KV_WIDTH]), 0.0)
    vw_ref[...] = per_group(zkv[:, 5 * KV_WIDTH:6 * KV_WIDTH], ones_col)
    gate_ref[...] = jax.nn.sigmoid(_dot(u, wgn_ref[...]))


def _const_spec(shape):
    nd = len(shape)
    return pl.BlockSpec(shape, lambda *_: (0,) * nd, pipeline_mode=pl.Buffered(1))


def _head_call(x2, pos, invf, n1, wg, wu, wd, n2, wq, wkv, wgn, seq):
    t = x2.shape[0]
    tm = HEAD_TM
    assert seq % tm == 0 and seq & (seq - 1) == 0

    def row(width):
        return pl.BlockSpec((tm, width), lambda i: (i, 0))

    def out(width, dtype):
        return jax.ShapeDtypeStruct((t, width), dtype)

    wide = NSA_KV_HEADS * LANES
    chunk = CMP_STRIDE * KV_WIDTH
    chunked = pl.BlockSpec((tm // CMP_STRIDE, chunk), lambda i: (i, 0))
    chunked_shape = jax.ShapeDtypeStruct((t // CMP_STRIDE, chunk), BF16)
    return pl.pallas_call(
        functools.partial(_head_kernel, seq=seq),
        grid=(t // tm,),
        in_specs=[row(D_MODEL), row(1), _const_spec((1, LANES)), _const_spec((1, D_MODEL)),
                  _const_spec((D_MODEL, D_FF)), _const_spec((D_MODEL, D_FF)), _const_spec((D_FF, D_MODEL)),
                  _const_spec((1, D_MODEL)), _const_spec((D_MODEL, NSA_WIDTH)),
                  _const_spec((D_MODEL, 6 * KV_WIDTH)), _const_spec((D_MODEL, NSA_KV_HEADS * GATE_PAD))],
        out_specs=[row(D_MODEL), row(NSA_WIDTH), chunked, chunked] + [row(wide)] * 4
                  + [row(NSA_KV_HEADS * GATE_PAD)],
        out_shape=[out(D_MODEL, F32), out(NSA_WIDTH, BF16), chunked_shape, chunked_shape]
                  + [out(wide, BF16)] * 4 + [out(NSA_KV_HEADS * GATE_PAD, F32)],
        scratch_shapes=[pltpu.VMEM((2, tm, KV_WIDTH), F32)],
        compiler_params=pltpu.CompilerParams(dimension_semantics=("arbitrary",),
                                             vmem_limit_bytes=VMEM_LIMIT),
        name="head",
    )(x2, pos, invf, n1, wg, wu, wd, n2, wq, wkv, wgn)


def _compress_kernel(kx_ref, vx_ref,
                     kpa_ref, kpb_ref, kwa_ref, kwb_ref, kb1_ref, kw2_ref,
                     vpa_ref, vpb_ref, vwa_ref, vwb_ref, vb1_ref, vw2_ref,
                     kc_ref, vc_ref):
    def one(x_ref, pa_ref, pb_ref, wa_ref, wb_ref, b1_ref, w2_ref):
        x = x_ref[0].astype(F32)
        first = _dot((x + pa_ref[...]).astype(BF16), wa_ref[...])
        second = _dot((x + pb_ref[...]).astype(BF16), wb_ref[...])
        n = first.shape[0]
        hidden = first + pltpu.roll(second, n - 1, 0) + b1_ref[...]
        return _dot(jax.nn.gelu(hidden).astype(BF16), w2_ref[...])

    kc_ref[0] = one(kx_ref, kpa_ref, kpb_ref, kwa_ref, kwb_ref, kb1_ref, kw2_ref).astype(BF16)
    vc_ref[0] = one(vx_ref, vpa_ref, vpb_ref, vwa_ref, vwb_ref, vb1_ref, vw2_ref).astype(BF16)


def _compress_call(kx, vx, kparams, vparams):
    b, nrow, width = kx.shape
    hid = NSA_KV_HEADS * CMP_HIDDEN
    wide = NSA_KV_HEADS * LANES
    x_spec = pl.BlockSpec((1, nrow, width), lambda i: (i, 0, 0))
    w_specs = [_const_spec((1, width)), _const_spec((1, width)), _const_spec((width, hid)),
               _const_spec((width, hid)), _const_spec((1, hid)), _const_spec((hid, wide))]
    o_spec = pl.BlockSpec((1, nrow, wide), lambda i: (i, 0, 0))
    o_shape = jax.ShapeDtypeStruct((b, nrow, wide), BF16)
    return pl.pallas_call(
        _compress_kernel,
        grid=(b,),
        in_specs=[x_spec, x_spec] + w_specs + w_specs,
        out_specs=[o_spec, o_spec],
        out_shape=[o_shape, o_shape],
        compiler_params=pltpu.CompilerParams(dimension_semantics=("arbitrary",),
                                             vmem_limit_bytes=VMEM_LIMIT),
        name="compress",
    )(kx, vx, *kparams, *vparams)


def _attn_kernel(q_ref, qn_ref, gate_ref, kc_ref, vc_ref, ks_ref, vs_ref, kw_ref, vw_ref, ovl_ref, o_ref,
                 qa_ref, oc_ref):
    qb = Q_BLOCK
    rows = HPG * qb
    groups = range(NSA_KV_HEADS)
    step = pl.program_id(1)
    slot = step % 2
    n_cmp = kc_ref.shape[1]
    n_blk = ovl_ref.shape[0]

    def tile(g):
        return slice(g * LANES, (g + 1) * LANES)

    lane = lax.broadcasted_iota(jnp.int32, (qb, LANES), 1)
    is_dim = lane < HEAD_DIM

    def positions(blk):
        return blk * qb + lax.broadcasted_iota(jnp.int32, (qb, 1), 0)

    def split_heads(block_ref):
        q_all = block_ref[0].astype(F32)
        q_heads = []
        for g in groups:
            heads = []
            for hh in range(HPG):
                pair = q_all[:, (g * HPG + hh) // 2 * LANES:((g * HPG + hh) // 2 + 1) * LANES]
                heads.append(pair if hh % 2 == 0 else pltpu.roll(pair, HEAD_DIM, 1))
            q_heads.append(heads)
        return q_heads

    def stack_heads(heads):
        return jnp.concatenate([jnp.where(is_dim, qh, 0.0) for qh in heads], axis=0).astype(BF16)

    def mask_heads(mask, s):
        s3 = s.reshape(HPG, qb, s.shape[-1])
        return jnp.where(mask[None], s3, NEG).reshape(rows, s.shape[-1])

    def normalise(acc):
        return acc * (1.0 / acc[:, HEAD_DIM:HEAD_DIM + 1])

    def select(block_ref, blk, dst):
        tq = positions(blk)
        q_heads = split_heads(block_ref)

        c_last = lax.broadcasted_iota(jnp.int32, (qb, n_cmp), 1) * CMP_STRIDE + (CMP_LEN - 1)
        c_mask = c_last <= tq
        s_c = [mask_heads(c_mask, _dot_nt(stack_heads(q_heads[g]), kc_ref[0, :, tile(g)])) for g in groups]
        sees_cmp = jnp.where(tq >= CMP_LEN - 1, 1.0, 0.0)[None]
        p_c = []
        for g in groups:
            e_c = jnp.exp2(s_c[g] - jnp.max(s_c[g], axis=-1, keepdims=True)).reshape(HPG, qb, n_cmp)
            p_c.append(e_c * (sees_cmp / jnp.sum(e_c, axis=-1, keepdims=True)))
            oc_ref[dst, g] = _dot(p_c[g].reshape(rows, n_cmp).astype(BF16), vc_ref[0, :, tile(g)])

        ovl = ovl_ref[...]
        imp = []
        for g in groups:
            p_sum = jnp.sum(p_c[g], axis=0)
            p_hi = p_sum.astype(BF16)
            p_lo = (p_sum - p_hi.astype(F32)).astype(BF16)
            imp.append(_dot_nt(ovl, p_hi) + _dot_nt(ovl, p_lo))
        imp = jnp.concatenate(imp, axis=1)
        nq = NSA_KV_HEADS * qb
        s_idx = lax.broadcasted_iota(jnp.int32, (n_blk, nq), 0)
        tq_row = blk * qb + (lax.broadcasted_iota(jnp.int32, (1, nq), 1) & (qb - 1))
        blk_q = tq_row >> SEL_SHIFT
        valid = s_idx <= blk_q
        forced = (s_idx == 0) | (valid & (s_idx > blk_q - N_LOCAL))
        score = jnp.where(forced, -jnp.inf, jnp.where(valid, imp, INVALID_SCORE))
        chosen = forced
        s_idx_f = s_idx.astype(F32)
        for _ in range(N_SEL - 1 - N_LOCAL):
            top = jnp.max(score, axis=0, keepdims=True)
            first = jnp.min(jnp.where(score == top, s_idx_f, float(n_blk)), axis=0, keepdims=True)
            hit = s_idx_f == first
            chosen = chosen | hit
            score = jnp.where(hit, -jnp.inf, score)
        bias = jnp.where(chosen & valid, 0.0, NEG)

        for g in groups:
            bias_g = bias[:, g * qb:(g + 1) * qb].T
            for w in range(n_blk // BIAS_LANES):
                b_w = bias_g[:, w * BIAS_LANES:(w + 1) * BIAS_LANES]
                b_w = jnp.concatenate([b_w, b_w], axis=1)
                qa_ref[dst, g, w] = jnp.concatenate([jnp.where(is_dim, qh, b_w) for qh in q_heads[g]],
                                                    axis=0).astype(BF16)

    @pl.when(step == 0)
    def _():
        select(q_ref, 0, 0)

    qs = step * qb
    tq = positions(step)
    ck = SEL_CHUNK
    span_keys = SEL_SPAN_CHUNKS * ck
    spans_per_bias = BIAS_LANES * SEL_LEN // span_keys
    own_chunk = (qs // ck) % SEL_SPAN_CHUNKS

    def sel_span(t, carry, n_spans, causal, which=groups):
        carry = list(carry)
        for g in which:
            m_prev, acc = carry[g]
            for u in range(n_spans):
                qa = qa_ref[slot, g, (n_spans * t + u) // spans_per_bias]
                scores = []
                for c in range(SEL_SPAN_CHUNKS):
                    c_eff = (own_chunk + c) % SEL_SPAN_CHUNKS if causal else c
                    start = pl.multiple_of((n_spans * t + u) * span_keys + c_eff * ck, ck)
                    s = _dot_nt(qa, ks_ref[0, pl.ds(start, ck), tile(g)])
                    if causal and c == 0:
                        s = mask_heads(start + lax.broadcasted_iota(jnp.int32, (qb, ck), 1) <= tq, s)
                    scores.append((s, start))
                for s, start in scores:
                    m_new = jnp.maximum(m_prev, jnp.max(s, axis=-1, keepdims=True))
                    pv = _dot(jnp.exp2((s - m_new).astype(BF16)), vs_ref[0, pl.ds(start, ck), tile(g)])
                    m_prev, acc = m_new, jnp.exp2(m_prev - m_new) * acc + pv
            carry[g] = (m_prev, acc)
        return tuple(carry)

    n_plain = (qs + qb - 1) // span_keys
    carry = ((jnp.full((rows, 1), NEG, F32), jnp.zeros((rows, LANES), F32)),) * NSA_KV_HEADS
    n_wide = n_plain // SEL_UNROLL
    if SEL_UNROLL > 1:
        carry = lax.fori_loop(0, n_wide, functools.partial(sel_span, n_spans=SEL_UNROLL, causal=False), carry)
    carry = lax.fori_loop(n_wide * SEL_UNROLL, n_plain, functools.partial(sel_span, n_spans=1, causal=False),
                          carry)

    select(qn_ref, step + 1, 1 - slot)

    span = WINDOW + qb
    w_start = pl.multiple_of(jnp.maximum(qs - WINDOW, 0), qb)
    kpos_w = w_start + lax.broadcasted_iota(jnp.int32, (qb, span), 1)
    w_mask = (kpos_w <= tq) & (kpos_w > tq - WINDOW)
    q_heads = split_heads(q_ref)
    o_w = []
    for g in groups:
        s_w = mask_heads(w_mask, _dot_nt(stack_heads(q_heads[g]), kw_ref[0, pl.ds(w_start, span), tile(g)]))
        e_w = jnp.exp2((s_w - jnp.max(s_w, axis=-1, keepdims=True)).astype(BF16))
        o_w.append(normalise(_dot(e_w, vw_ref[0, pl.ds(w_start, span), tile(g)])))

    for g in groups:
        carry = sel_span(n_plain, carry, 1, True, which=(g,))
    o_s = [normalise(carry[g][1]) for g in groups]

    for g in groups:
        gates = gate_ref[0, :, tile(g)]
        o_c = oc_ref[slot, g]
        mixed = []
        for hh in range(HPG):
            sl = slice(hh * qb, (hh + 1) * qb)
            g_c = gates[:, 3 * hh + 0:3 * hh + 1]
            g_s = gates[:, 3 * hh + 1:3 * hh + 2]
            g_w = gates[:, 3 * hh + 2:3 * hh + 3]
            mixed.append(g_c * o_c[sl] + g_s * o_s[g][sl] + g_w * o_w[g][sl])
        for pr in range(HPG // 2):
            odd = pltpu.roll(mixed[2 * pr + 1], HEAD_DIM, 1)
            col = (g * HPG // 2 + pr) * LANES
            o_ref[0, :, col:col + LANES] = jnp.where(is_dim, mixed[2 * pr], odd).astype(BF16)


def _attn_call(q, gates, kc, vc, ks, vs, kw, vw, overlap_t):
    b, s, _ = q.shape
    n_cmp = kc.shape[1]
    n_blk = s // SEL_LEN
    span_keys = SEL_SPAN_CHUNKS * SEL_CHUNK
    assert n_blk % BIAS_LANES == 0 and BIAS_LANES * SEL_LEN % span_keys == 0 and s % span_keys == 0
    wide = NSA_KV_HEADS * LANES
    seq_spec = pl.BlockSpec((1, s, wide), lambda bi, i: (bi, 0, 0))
    cmp_spec = pl.BlockSpec((1, n_cmp, wide), lambda bi, i: (bi, 0, 0))
    n_steps = s // Q_BLOCK
    rows = HPG * Q_BLOCK
    return pl.pallas_call(
        _attn_kernel,
        grid=(b, n_steps),
        in_specs=[pl.BlockSpec((1, Q_BLOCK, NSA_WIDTH), lambda bi, i: (bi, i, 0)),
                  pl.BlockSpec((1, Q_BLOCK, NSA_WIDTH), lambda bi, i: (bi, jnp.minimum(i + 1, n_steps - 1), 0)),
                  pl.BlockSpec((1, Q_BLOCK, NSA_KV_HEADS * GATE_PAD), lambda bi, i: (bi, i, 0)),
                  cmp_spec, cmp_spec, seq_spec, seq_spec, seq_spec, seq_spec,
                  _const_spec(overlap_t.shape)],
        out_specs=pl.BlockSpec((1, Q_BLOCK, NSA_WIDTH), lambda bi, i: (bi, i, 0)),
        out_shape=jax.ShapeDtypeStruct((b, s, NSA_WIDTH), BF16),
        scratch_shapes=[pltpu.VMEM((2, NSA_KV_HEADS, n_blk // BIAS_LANES, rows, LANES), BF16),
                        pltpu.VMEM((2, NSA_KV_HEADS, rows, LANES), F32)],
        compiler_params=pltpu.CompilerParams(dimension_semantics=("arbitrary",) * 2,
                                             vmem_limit_bytes=VMEM_LIMIT),
        name="attention",
    )(q, q, gates, kc, vc, ks, vs, kw, vw, overlap_t)


def _tail_kernel(h1_ref, on_ref, p_ref, nm_ref, wc_ref, wga_ref, wgb_ref, cw_ref, cb_ref,
                 wa_ref, wb_ref, wo_ref, n2_ref, wg_ref, wu_ref, wd_ref,
                 np_ref, pg_ref, pp_ref, nf_ref, out_ref, carry_ref, *, tiles_per_seq):
    i = pl.program_id(0)
    h1 = h1_ref[...]
    tm = h1.shape[0]
    u = _rms(h1, nm_ref[...]).astype(BF16)

    zc = _dot(u, wc_ref[...])
    gate_b = zc[:, :CONV_WIDTH]
    v = zc[:, CONV_WIDTH:2 * CONV_WIDTH] * zc[:, 2 * CONV_WIDTH:]

    @pl.when(i % tiles_per_seq == 0)
    def _():
        carry_ref[...] = jnp.zeros_like(carry_ref)

    prev = carry_ref[...]
    row = lax.broadcasted_iota(jnp.int32, (tm, CONV_WIDTH), 0)
    v1 = jnp.where(row == 0, prev[SUBLANES - 1:SUBLANES], pltpu.roll(v, 1, 0))
    v2 = jnp.where(row == 0, prev[SUBLANES - 2:SUBLANES - 1],
                   jnp.where(row == 1, prev[SUBLANES - 1:SUBLANES], pltpu.roll(v, 2, 0)))
    carry_ref[...] = v[tm - SUBLANES:]
    cw = cw_ref[...]
    y = cw[0:1] * v2 + cw[1:2] * v1 + cw[2:3] * v
    o_conv = (gate_b * (y + cb_ref[...])).astype(BF16)

    merged = (jax.nn.sigmoid(_dot(u, wga_ref[...])) * _dot(on_ref[...], wa_ref[...])
              + jax.nn.sigmoid(_dot(u, wgb_ref[...])) * _dot(o_conv, wb_ref[...]))
    h2 = h1 + _dot(merged.astype(BF16), wo_ref[...])

    h3 = h2 + 0.5 * _swiglu_half(_rms(h2, n2_ref[...]).astype(BF16), wg_ref, wu_ref, wd_ref)

    gate_p = jax.nn.sigmoid(_dot(_rms(h3, np_ref[...]).astype(BF16), pg_ref[...]))
    h4 = h3 + gate_p * _dot(p_ref[...].astype(BF16), pp_ref[...])
    out_ref[...] = _rms(h4, nf_ref[...])


def _tail_call(h1, o_nsa, p2, seq, nm, wc, wga, wgb, cw, cb, wa, wb, wo, n2, wg, wu, wd, npl, pg, pp, nf):
    t = h1.shape[0]
    tm = TAIL_TM
    assert CONV_K - 1 <= SUBLANES and seq % tm == 0

    def row(width):
        return pl.BlockSpec((tm, width), lambda i: (i, 0))

    consts = [nm, wc, wga, wgb, cw, cb, wa, wb, wo, n2, wg, wu, wd, npl, pg, pp, nf]
    return pl.pallas_call(
        functools.partial(_tail_kernel, tiles_per_seq=seq // tm),
        grid=(t // tm,),
        in_specs=[row(D_MODEL), row(NSA_WIDTH), row(PLE_DIM)] + [_const_spec(c.shape) for c in consts],
        out_specs=row(D_MODEL),
        out_shape=jax.ShapeDtypeStruct((t, D_MODEL), F32),
        scratch_shapes=[pltpu.VMEM((SUBLANES, CONV_WIDTH), F32)],
        compiler_params=pltpu.CompilerParams(dimension_semantics=("arbitrary",),
                                             vmem_limit_bytes=VMEM_LIMIT),
        name="tail",
    )(h1, o_nsa, p2, *consts)


def _compress_params(pos, w1, b1, w2):
    eye = jnp.eye(NSA_KV_HEADS, dtype=F32)
    half = CMP_LEN // 2
    w1r = w1.reshape(CMP_LEN, HEAD_DIM, CMP_HIDDEN)

    def expand(w):
        return jnp.einsum('ldn,gh->lgdhn', w, eye).reshape(half * KV_WIDTH, NSA_KV_HEADS * CMP_HIDDEN).astype(BF16)

    def pos_row(pp):
        return jnp.broadcast_to(pp[:, None, :], (half, NSA_KV_HEADS, HEAD_DIM)).reshape(1, half * KV_WIDTH)

    w2p = jnp.pad(w2, ((0, 0), (0, LANES - HEAD_DIM)))
    w2b = jnp.einsum('nd,gh->gnhd', w2p, eye).reshape(NSA_KV_HEADS * CMP_HIDDEN, NSA_KV_HEADS * LANES).astype(BF16)
    b1t = jnp.tile(b1, NSA_KV_HEADS).reshape(1, NSA_KV_HEADS * CMP_HIDDEN)
    return (pos_row(pos[:half]), pos_row(pos[half:]), expand(w1r[:half]), expand(w1r[half:]), b1t, w2b)


def kernel(x, p, positions, ffn1_norm, ffn1_w_gate, ffn1_w_up, ffn1_w_down, mix_norm, w_in, cmp_k_pos, cmp_k_w1, cmp_k_b1, cmp_k_w2, cmp_v_pos, cmp_v_w1, cmp_v_b1, cmp_v_w2, conv_w, conv_b, w_proj_nsa, w_proj_conv, w_out, ffn2_norm, ffn2_w_gate, ffn2_w_up, ffn2_w_down, ple_norm, ple_w_gate, ple_w_proj, final_norm):
    b, s, d = x.shape
    t = b * s
    depth = ffn1_norm.shape[0]
    assert depth == 1 and d == D_MODEL and s % SEL_CHUNK == 0

    half = HEAD_DIM // 2
    inv_freq = ROPE_THETA ** (-jnp.arange(half, dtype=F32) / half)
    invf = jnp.tile(inv_freq, LANES // half).reshape(1, LANES)
    pos = positions.astype(F32).reshape(t, 1)

    n_cmp_rows = s // CMP_STRIDE
    n_blk = s // SEL_LEN
    c_start = jnp.arange(n_cmp_rows) * CMP_STRIDE
    s_start = jnp.arange(n_blk) * SEL_LEN
    overlap_t = (jnp.clip(jnp.minimum(c_start[None, :] + CMP_LEN, s_start[:, None] + SEL_LEN)
                          - jnp.maximum(c_start[None, :], s_start[:, None]), 0, None).astype(F32)
                 / CMP_LEN).astype(BF16)

    row = lambda v: v.reshape(1, -1)
    h = x.reshape(t, d)
    for li in range(depth):
        wi = w_in[li]
        o = 0
        parts = []
        for sz in (NSA_WIDTH, 6 * KV_WIDTH, 3 * NSA_HEADS, 3 * CONV_WIDTH, D_MODEL, D_MODEL):
            parts.append(wi[:, o:o + sz])
            o += sz
        wq, wkv, wgn, wc, wga, wgb = parts
        per_group = 3 * HPG
        wgn = jnp.pad(wgn.reshape(d, NSA_KV_HEADS, per_group),
                      ((0, 0), (0, 0), (0, GATE_PAD - per_group))).reshape(d, NSA_KV_HEADS * GATE_PAD)

        h1, q, kc_r, vc_r, ks, vs, kw, vw, gates = _head_call(
            h, pos, invf, row(ffn1_norm[li]), ffn1_w_gate[li].astype(BF16), ffn1_w_up[li].astype(BF16),
            ffn1_w_down[li].astype(BF16), row(mix_norm[li]), wq.astype(BF16), wkv.astype(BF16),
            wgn.astype(BF16), s)

        chunk = CMP_STRIDE * KV_WIDTH
        kc, vc = _compress_call(
            kc_r.reshape(b, n_cmp_rows, chunk), vc_r.reshape(b, n_cmp_rows, chunk),
            _compress_params(cmp_k_pos[li], cmp_k_w1[li], cmp_k_b1[li], cmp_k_w2[li]),
            _compress_params(cmp_v_pos[li], cmp_v_w1[li], cmp_v_b1[li], cmp_v_w2[li]))

        seq3 = lambda a: a.reshape(b, s, a.shape[-1])
        o_nsa = _attn_call(seq3(q), seq3(gates), kc, vc, seq3(ks), seq3(vs), seq3(kw), seq3(vw), overlap_t)

        h = _tail_call(
            h1, o_nsa.reshape(t, NSA_WIDTH), p[li].reshape(t, PLE_DIM), s,
            row(mix_norm[li]), wc.astype(BF16), wga.astype(BF16), wgb.astype(BF16),
            conv_w[li], row(conv_b[li]), w_proj_nsa[li].astype(BF16), w_proj_conv[li].astype(BF16),
            w_out[li].astype(BF16), row(ffn2_norm[li]), ffn2_w_gate[li].astype(BF16),
            ffn2_w_up[li].astype(BF16), ffn2_w_down[li].astype(BF16), row(ple_norm[li]),
            ple_w_gate[li].astype(BF16), ple_w_proj[li].astype(BF16), row(final_norm))
    return h.reshape(b, s, d)
```

```python
import functools
import math

import jax
import jax.numpy as jnp
from jax import lax
from jax.experimental import pallas as pl
from jax.experimental.pallas import tpu as pltpu

D_MODEL = 1024
D_FF = 2816
PLE_DIM = 256
EPS = 1e-6
ROPE_THETA = 10000.0

NSA_HEADS = 8
NSA_KV_HEADS = 2
HEAD_DIM = 64
HPG = NSA_HEADS // NSA_KV_HEADS
NSA_WIDTH = NSA_HEADS * HEAD_DIM
KV_WIDTH = NSA_KV_HEADS * HEAD_DIM
CMP_LEN = 32
CMP_STRIDE = 16
CMP_HIDDEN = 256
SEL_LEN = 64
N_SEL = 16
N_LOCAL = 2
WINDOW = 512
Q_BLOCK = 128
FORCED_SCORE = 1e4
INVALID_SCORE = -1e4
CONV_WIDTH = 512
CONV_K = 3
MIX_COLS = 3 * CONV_WIDTH + 2 * D_MODEL

SEL_SHIFT = SEL_LEN.bit_length() - 1

LANES = 128
SUBLANES = 8
VMEM_LIMIT = 58 * 1024 * 1024

HEAD_TM = 1024
TAIL_TM = 512
MXU_DIM = 256
FF_CHUNKS = (6 * MXU_DIM, 5 * MXU_DIM)
SEL_CHUNK = 512
SEL_SPAN_CHUNKS = 2
SEL_UNROLL = 2
GATE_PAD = LANES
BIAS_LANES = LANES - HEAD_DIM
ROPE_PACK = LANES // (HEAD_DIM // 2)
NEG = -(2.0 ** 100)
Q_SCALE = HEAD_DIM ** -0.5 * math.log2(math.e)

BF16 = jnp.bfloat16
F32 = jnp.float32


def _rms(x, g):
    return x * lax.rsqrt(jnp.mean(x * x, axis=-1, keepdims=True) + EPS) * g


def _dot(a, b):
    return jnp.dot(a, b, preferred_element_type=F32)


def _dot_nt(a, b):
    return lax.dot_general(a, b, (((1,), (1,)), ((), ())), preferred_element_type=F32)


def _swiglu_half(xn, wg_ref, wu_ref, wd_ref):
    acc = jnp.zeros((xn.shape[0], D_MODEL), F32)
    assert sum(FF_CHUNKS) == D_FF
    lo = 0
    for width in FF_CHUNKS:
        sl = slice(lo, lo + width)
        lo += width
        gate = _dot(xn, wg_ref[:, sl])
        up = _dot(xn, wu_ref[:, sl])
        act = (gate * jax.nn.sigmoid(gate) * up).astype(BF16)
        acc = acc + _dot(act, wd_ref[sl, :])
    return acc


def _head_kernel(x_ref, pos_ref, invf_ref, n1_ref, wg_ref, wu_ref, wd_ref, n2_ref,
                 wqkv_ref, wgn_ref,
                 h1_ref, q_ref, kc_ref, vc_ref, ks_ref, vs_ref, kw_ref, vw_ref, gate_ref, stage_ref, *, seq):
    x = x_ref[...]
    tm = x.shape[0]
    xn = _rms(x, n1_ref[...]).astype(BF16)

    def low_half(width):
        lane = lax.broadcasted_iota(jnp.int32, (tm, width), 1)
        return (lane & (HEAD_DIM - 1)) < (HEAD_DIM // 2)

    h1 = x + 0.5 * _swiglu_half(xn, wg_ref, wu_ref, wd_ref)
    h1_ref[...] = h1
    u = _rms(h1, n2_ref[...]).astype(BF16)

    half = HEAD_DIM // 2
    quarter = tm // ROPE_PACK
    qlane = lax.broadcasted_iota(jnp.int32, (quarter, LANES), 1)
    pos_rows = pos_ref[...]
    pos_cols = jnp.concatenate([pos_rows] * (LANES // pos_rows.shape[0]), axis=0).T
    cols_per_pack = quarter // LANES
    plane = lax.broadcasted_iota(jnp.int32, (LANES, LANES), 1)
    pieces = []
    for r in range(cols_per_pack):
        piece = jnp.broadcast_to(pos_cols[:, r:r + 1], (LANES, LANES))
        for p in range(1, ROPE_PACK):
            col = p * cols_per_pack + r
            piece = jnp.where(plane >= p * half, pos_cols[:, col:col + 1], piece)
        pieces.append(piece)
    packed_pos = jnp.concatenate(pieces, axis=0)
    ang = packed_pos * invf_ref[...]

    def unpack(t):
        out = []
        for p in range(ROPE_PACK):
            v = t if p == 0 else pltpu.roll(t, LANES - p * half, 1)
            span = half
            while span < LANES:
                v = jnp.where(qlane < span, v, pltpu.roll(v, span, 1))
                span *= 2
            out.append(v)
        return jnp.concatenate(out, axis=0)

    cos = unpack(jnp.cos(ang))
    sin = unpack(jnp.sin(ang))
    sin_signed = jnp.where(low_half(LANES), -sin, sin)

    def rope(z):
        width = z.shape[1]
        reps = width // LANES
        c = jnp.concatenate([cos] * reps, axis=1) if reps > 1 else cos
        s = jnp.concatenate([sin_signed] * reps, axis=1) if reps > 1 else sin_signed
        partner = jnp.where(low_half(width), pltpu.roll(z, width - HEAD_DIM // 2, 1),
                            pltpu.roll(z, HEAD_DIM // 2, 1))
        return z * c + partner * s

    zq = _dot(u, wqkv_ref[:, :NSA_WIDTH])
    q_ref[...] = (rope(zq) * Q_SCALE).astype(BF16)

    lane = lax.broadcasted_iota(jnp.int32, (tm, LANES), 1)
    is_dim = lane < HEAD_DIM
    ones_col = jnp.where(lane == HEAD_DIM, 1.0, 0.0)
    seq_pos = (pl.program_id(0) * tm + lax.broadcasted_iota(jnp.int32, (tm, LANES), 0)) & (seq - 1)
    blk_onehot = jnp.where(lane - HEAD_DIM == ((seq_pos >> SEL_SHIFT) & (BIAS_LANES - 1)), 1.0, 0.0)

    def per_group(z, extra):
        g0 = jnp.where(is_dim, z, extra)
        g1 = jnp.where(is_dim, pltpu.roll(z, HEAD_DIM, 1), extra)
        return jnp.concatenate([g0, g1], axis=1).astype(BF16)

    zkv = _dot(u, wqkv_ref[:, NSA_WIDTH:])
    stage_ref[0] = rope(zkv[:, 0 * KV_WIDTH:1 * KV_WIDTH])
    stage_ref[1] = zkv[:, 1 * KV_WIDTH:2 * KV_WIDTH]
    for which, dst_ref in enumerate((kc_ref, vc_ref)):
        for l in range(CMP_STRIDE):
            rows_l = stage_ref[which, pl.ds(l, tm // CMP_STRIDE, stride=CMP_STRIDE), :]
            dst_ref[:, l * KV_WIDTH:(l + 1) * KV_WIDTH] = rows_l.astype(BF16)
    ks_ref[...] = per_group(rope(zkv[:, 2 * KV_WIDTH:3 * KV_WIDTH]), blk_onehot)
    vs_ref[...] = per_group(zkv[:, 3 * KV_WIDTH:4 * KV_WIDTH], ones_col)
    kw_ref[...] = per_group(rope(zkv[:, 4 * KV_WIDTH:5 * KV_WIDTH]), 0.0)
    vw_ref[...] = per_group(zkv[:, 5 * KV_WIDTH:6 * KV_WIDTH], ones_col)
    gate_ref[...] = jax.nn.sigmoid(_dot(u, wgn_ref[...]))


def _const_spec(shape):
    nd = len(shape)
    return pl.BlockSpec(shape, lambda *_: (0,) * nd, pipeline_mode=pl.Buffered(1))


def _head_call(x2, pos, invf, n1, wg, wu, wd, n2, wqkv, wgn, seq):
    t = x2.shape[0]
    tm = HEAD_TM
    assert seq % tm == 0 and seq & (seq - 1) == 0 and tm % (SUBLANES * LANES) == 0

    def row(width):
        return pl.BlockSpec((tm, width), lambda i: (i, 0))

    def out(width, dtype):
        return jax.ShapeDtypeStruct((t, width), dtype)

    wide = NSA_KV_HEADS * LANES
    chunk = CMP_STRIDE * KV_WIDTH
    chunked = pl.BlockSpec((tm // CMP_STRIDE, chunk), lambda i: (i, 0))
    chunked_shape = jax.ShapeDtypeStruct((t // CMP_STRIDE, chunk), BF16)
    return pl.pallas_call(
        functools.partial(_head_kernel, seq=seq),
        grid=(t // tm,),
        in_specs=[row(D_MODEL), pl.BlockSpec((tm // LANES, LANES), lambda i: (i, 0)),
                  _const_spec((1, LANES)), _const_spec((1, D_MODEL)),
                  _const_spec((D_MODEL, D_FF)), _const_spec((D_MODEL, D_FF)), _const_spec((D_FF, D_MODEL)),
                  _const_spec((1, D_MODEL)), _const_spec((D_MODEL, NSA_WIDTH + 6 * KV_WIDTH)),
                  _const_spec((D_MODEL, NSA_KV_HEADS * GATE_PAD))],
        out_specs=[row(D_MODEL), row(NSA_WIDTH), chunked, chunked] + [row(wide)] * 4
                  + [row(NSA_KV_HEADS * GATE_PAD)],
        out_shape=[out(D_MODEL, F32), out(NSA_WIDTH, BF16), chunked_shape, chunked_shape]
                  + [out(wide, BF16)] * 4 + [out(NSA_KV_HEADS * GATE_PAD, F32)],
        scratch_shapes=[pltpu.VMEM((2, tm, KV_WIDTH), F32)],
        compiler_params=pltpu.CompilerParams(dimension_semantics=("arbitrary",),
                                             vmem_limit_bytes=VMEM_LIMIT),
        name="head",
    )(x2, pos, invf, n1, wg, wu, wd, n2, wqkv, wgn)


def _compress_kernel(kx_ref, vx_ref,
                     kpa_ref, kpb_ref, kwa_ref, kwb_ref, kb1_ref, kw2_ref,
                     vpa_ref, vpb_ref, vwa_ref, vwb_ref, vb1_ref, vw2_ref,
                     kc_ref, vc_ref):
    def one(x_ref, pa_ref, pb_ref, wa_ref, wb_ref, b1_ref, w2_ref):
        x = x_ref[0].astype(F32)
        first = _dot((x + pa_ref[...]).astype(BF16), wa_ref[...])
        second = _dot((x + pb_ref[...]).astype(BF16), wb_ref[...])
        n = first.shape[0]
        hidden = first + pltpu.roll(second, n - 1, 0) + b1_ref[...]
        return _dot(jax.nn.gelu(hidden).astype(BF16), w2_ref[...])

    kc_ref[0] = one(kx_ref, kpa_ref, kpb_ref, kwa_ref, kwb_ref, kb1_ref, kw2_ref).astype(BF16)
    vc_ref[0] = one(vx_ref, vpa_ref, vpb_ref, vwa_ref, vwb_ref, vb1_ref, vw2_ref).astype(BF16)


def _compress_call(kx, vx, kparams, vparams):
    b, nrow, width = kx.shape
    hid = NSA_KV_HEADS * CMP_HIDDEN
    wide = NSA_KV_HEADS * LANES
    x_spec = pl.BlockSpec((1, nrow, width), lambda i: (i, 0, 0))
    w_specs = [_const_spec((1, width)), _const_spec((1, width)), _const_spec((width, hid)),
               _const_spec((width, hid)), _const_spec((1, hid)), _const_spec((hid, wide))]
    o_spec = pl.BlockSpec((1, nrow, wide), lambda i: (i, 0, 0))
    o_shape = jax.ShapeDtypeStruct((b, nrow, wide), BF16)
    return pl.pallas_call(
        _compress_kernel,
        grid=(b,),
        in_specs=[x_spec, x_spec] + w_specs + w_specs,
        out_specs=[o_spec, o_spec],
        out_shape=[o_shape, o_shape],
        compiler_params=pltpu.CompilerParams(dimension_semantics=("arbitrary",),
                                             vmem_limit_bytes=VMEM_LIMIT),
        name="compress",
    )(kx, vx, *kparams, *vparams)


def _attn_kernel(q_ref, qn_ref, gate_ref, kc_ref, vc_ref, ks_ref, vs_ref, kw_ref, vw_ref, ovl_ref, o_ref,
                 qa_ref, oc_ref):
    qb = Q_BLOCK
    rows = HPG * qb
    groups = range(NSA_KV_HEADS)
    step = pl.program_id(1)
    slot = step % 2
    n_cmp = kc_ref.shape[1]
    n_blk = ovl_ref.shape[0]

    def tile(g):
        return slice(g * LANES, (g + 1) * LANES)

    lane = lax.broadcasted_iota(jnp.int32, (qb, LANES), 1)
    is_dim = lane < HEAD_DIM

    def positions(blk):
        return blk * qb + lax.broadcasted_iota(jnp.int32, (qb, 1), 0)

    def split_heads(block_ref):
        q_all = block_ref[0].astype(F32)
        q_heads = []
        for g in groups:
            heads = []
            for hh in range(HPG):
                pair = q_all[:, (g * HPG + hh) // 2 * LANES:((g * HPG + hh) // 2 + 1) * LANES]
                heads.append(pair if hh % 2 == 0 else pltpu.roll(pair, HEAD_DIM, 1))
            q_heads.append(heads)
        return q_heads

    def stack_heads(heads):
        return jnp.concatenate([jnp.where(is_dim, qh, 0.0) for qh in heads], axis=0).astype(BF16)

    def mask_heads(mask, s):
        s3 = s.reshape(HPG, qb, s.shape[-1])
        return jnp.where(mask[None], s3, NEG).reshape(rows, s.shape[-1])

    def normalise(acc):
        return acc * (1.0 / acc[:, HEAD_DIM:HEAD_DIM + 1])

    def select(block_ref, blk, dst):
        tq = positions(blk)
        q_heads = split_heads(block_ref)

        c_last = lax.broadcasted_iota(jnp.int32, (qb, n_cmp), 1) * CMP_STRIDE + (CMP_LEN - 1)
        c_mask = c_last <= tq
        s_c = [mask_heads(c_mask, _dot_nt(stack_heads(q_heads[g]), kc_ref[0, :, tile(g)])) for g in groups]
        sees_cmp = jnp.where(tq >= CMP_LEN - 1, 1.0, 0.0)[None]
        p_c = []
        for g in groups:
            e_c = jnp.exp2(s_c[g] - jnp.max(s_c[g], axis=-1, keepdims=True)).reshape(HPG, qb, n_cmp)
            p_c.append(e_c * (sees_cmp / jnp.sum(e_c, axis=-1, keepdims=True)))
            oc_ref[dst, g] = _dot(p_c[g].reshape(rows, n_cmp).astype(BF16), vc_ref[0, :, tile(g)])

        ovl = ovl_ref[...]
        imp = []
        for g in groups:
            p_sum = jnp.sum(p_c[g], axis=0)
            p_hi = p_sum.astype(BF16)
            p_lo = (p_sum - p_hi.astype(F32)).astype(BF16)
            imp.append(_dot_nt(ovl, p_hi) + _dot_nt(ovl, p_lo))
        imp = jnp.concatenate(imp, axis=1)
        nq = NSA_KV_HEADS * qb
        s_idx = lax.broadcasted_iota(jnp.int32, (n_blk, nq), 0)
        tq_row = blk * qb + (lax.broadcasted_iota(jnp.int32, (1, nq), 1) & (qb - 1))
        blk_q = tq_row >> SEL_SHIFT
        valid = s_idx <= blk_q
        forced = (s_idx == 0) | (valid & (s_idx > blk_q - N_LOCAL))
        score = jnp.where(forced, -jnp.inf, jnp.where(valid, imp, INVALID_SCORE))
        chosen = forced
        s_idx_f = s_idx.astype(F32)
        for _ in range(N_SEL - 1 - N_LOCAL):
            top = jnp.max(score, axis=0, keepdims=True)
            first = jnp.min(jnp.where(score == top, s_idx_f, float(n_blk)), axis=0, keepdims=True)
            hit = s_idx_f == first
            chosen = chosen | hit
            score = jnp.where(hit, -jnp.inf, score)
        bias = jnp.where(chosen & valid, 0.0, NEG)

        for g in groups:
            bias_g = bias[:, g * qb:(g + 1) * qb].T
            for w in range(n_blk // BIAS_LANES):
                b_w = bias_g[:, w * BIAS_LANES:(w + 1) * BIAS_LANES]
                b_w = jnp.concatenate([b_w, b_w], axis=1)
                qa_ref[dst, g, w] = jnp.concatenate([jnp.where(is_dim, qh, b_w) for qh in q_heads[g]],
                                                    axis=0).astype(BF16)

    @pl.when(step == 0)
    def _():
        select(q_ref, 0, 0)

    qs = step * qb
    tq = positions(step)
    ck = SEL_CHUNK
    span_keys = SEL_SPAN_CHUNKS * ck
    spans_per_bias = BIAS_LANES * SEL_LEN // span_keys
    own_chunk = (qs // ck) % SEL_SPAN_CHUNKS

    def sel_span(t, carry, n_spans, causal, which=groups):
        carry = list(carry)
        for g in which:
            m_prev, acc = carry[g]
            for u in range(n_spans):
                qa = qa_ref[slot, g, (n_spans * t + u) // spans_per_bias]
                scores = []
                for c in range(SEL_SPAN_CHUNKS):
                    c_eff = (own_chunk + c) % SEL_SPAN_CHUNKS if causal else c
                    start = pl.multiple_of((n_spans * t + u) * span_keys + c_eff * ck, ck)
                    s = _dot_nt(qa, ks_ref[0, pl.ds(start, ck), tile(g)])
                    if causal and c == 0:
                        s = mask_heads(start + lax.broadcasted_iota(jnp.int32, (qb, ck), 1) <= tq, s)
                    scores.append((s, start))
                for s, start in scores:
                    m_new = jnp.maximum(m_prev, jnp.max(s, axis=-1, keepdims=True))
                    pv = _dot(jnp.exp2(s - m_new).astype(BF16), vs_ref[0, pl.ds(start, ck), tile(g)])
                    m_prev, acc = m_new, jnp.exp2(m_prev - m_new) * acc + pv
            carry[g] = (m_prev, acc)
        return tuple(carry)

    n_plain = (qs + qb - 1) // span_keys
    carry = ((jnp.full((rows, 1), NEG, F32), jnp.zeros((rows, LANES), F32)),) * NSA_KV_HEADS
    n_wide = n_plain // SEL_UNROLL
    if SEL_UNROLL > 1:
        carry = lax.fori_loop(0, n_wide, functools.partial(sel_span, n_spans=SEL_UNROLL, causal=False), carry)
    carry = lax.fori_loop(n_wide * SEL_UNROLL, n_plain, functools.partial(sel_span, n_spans=1, causal=False),
                          carry)

    select(qn_ref, step + 1, 1 - slot)

    span = WINDOW + qb
    w_start = pl.multiple_of(jnp.maximum(qs - WINDOW, 0), qb)
    kpos_w = w_start + lax.broadcasted_iota(jnp.int32, (qb, span), 1)
    w_mask = (kpos_w <= tq) & (kpos_w > tq - WINDOW)
    q_heads = split_heads(q_ref)
    o_w = []
    for g in groups:
        s_w = mask_heads(w_mask, _dot_nt(stack_heads(q_heads[g]), kw_ref[0, pl.ds(w_start, span), tile(g)]))
        e_w = jnp.exp2(s_w - jnp.max(s_w, axis=-1, keepdims=True)).astype(BF16)
        o_w.append(normalise(_dot(e_w, vw_ref[0, pl.ds(w_start, span), tile(g)])))

    for g in groups:
        carry = sel_span(n_plain, carry, 1, True, which=(g,))
    o_s = [normalise(carry[g][1]) for g in groups]

    for g in groups:
        gates = gate_ref[0, :, tile(g)]
        o_c = oc_ref[slot, g]
        mixed = []
        for hh in range(HPG):
            sl = slice(hh * qb, (hh + 1) * qb)
            g_c = gates[:, 3 * hh + 0:3 * hh + 1]
            g_s = gates[:, 3 * hh + 1:3 * hh + 2]
            g_w = gates[:, 3 * hh + 2:3 * hh + 3]
            mixed.append(g_c * o_c[sl] + g_s * o_s[g][sl] + g_w * o_w[g][sl])
        for pr in range(HPG // 2):
            odd = pltpu.roll(mixed[2 * pr + 1], HEAD_DIM, 1)
            col = (g * HPG // 2 + pr) * LANES
            o_ref[0, :, col:col + LANES] = jnp.where(is_dim, mixed[2 * pr], odd).astype(BF16)


def _attn_call(q, gates, kc, vc, ks, vs, kw, vw, overlap_t):
    b, s, _ = q.shape
    n_cmp = kc.shape[1]
    n_blk = s // SEL_LEN
    span_keys = SEL_SPAN_CHUNKS * SEL_CHUNK
    assert n_blk % BIAS_LANES == 0 and BIAS_LANES * SEL_LEN % span_keys == 0 and s % span_keys == 0
    wide = NSA_KV_HEADS * LANES
    seq_spec = pl.BlockSpec((1, s, wide), lambda bi, i: (bi, 0, 0), pipeline_mode=pl.Buffered(1))
    cmp_spec = pl.BlockSpec((1, n_cmp, wide), lambda bi, i: (bi, 0, 0), pipeline_mode=pl.Buffered(1))
    n_steps = s // Q_BLOCK
    rows = HPG * Q_BLOCK
    return pl.pallas_call(
        _attn_kernel,
        grid=(b, n_steps),
        in_specs=[pl.BlockSpec((1, Q_BLOCK, NSA_WIDTH), lambda bi, i: (bi, i, 0)),
                  pl.BlockSpec((1, Q_BLOCK, NSA_WIDTH), lambda bi, i: (bi, jnp.minimum(i + 1, n_steps - 1), 0)),
                  pl.BlockSpec((1, Q_BLOCK, NSA_KV_HEADS * GATE_PAD), lambda bi, i: (bi, i, 0)),
                  cmp_spec, cmp_spec, seq_spec, seq_spec, seq_spec, seq_spec,
                  _const_spec(overlap_t.shape)],
        out_specs=pl.BlockSpec((1, Q_BLOCK, NSA_WIDTH), lambda bi, i: (bi, i, 0)),
        out_shape=jax.ShapeDtypeStruct((b, s, NSA_WIDTH), BF16),
        scratch_shapes=[pltpu.VMEM((2, NSA_KV_HEADS, n_blk // BIAS_LANES, rows, LANES), BF16),
                        pltpu.VMEM((2, NSA_KV_HEADS, rows, LANES), F32)],
        compiler_params=pltpu.CompilerParams(dimension_semantics=("arbitrary",) * 2,
                                             vmem_limit_bytes=VMEM_LIMIT),
        name="attention",
    )(q, q, gates, kc, vc, ks, vs, kw, vw, overlap_t)


def _tail_kernel(h1_ref, on_ref, p_ref, nm_ref, wmix_ref, cw_ref, cb_ref,
                 wa_ref, wb_ref, wo_ref, n2_ref, wg_ref, wu_ref, wd_ref,
                 np_ref, pg_ref, pp_ref, nf_ref, out_ref, carry_ref, *, tiles_per_seq):
    i = pl.program_id(0)
    h1 = h1_ref[...]
    tm = h1.shape[0]
    u = _rms(h1, nm_ref[...]).astype(BF16)

    n_conv = 3 * CONV_WIDTH
    zc = _dot(u, wmix_ref[:, :n_conv])
    gate_b = zc[:, :CONV_WIDTH]
    v = zc[:, CONV_WIDTH:2 * CONV_WIDTH] * zc[:, 2 * CONV_WIDTH:]

    @pl.when(i % tiles_per_seq == 0)
    def _():
        carry_ref[...] = jnp.zeros_like(carry_ref)

    prev = carry_ref[...]
    row = lax.broadcasted_iota(jnp.int32, (tm, CONV_WIDTH), 0)
    v1 = jnp.where(row == 0, prev[SUBLANES - 1:SUBLANES], pltpu.roll(v, 1, 0))
    v2 = jnp.where(row == 0, prev[SUBLANES - 2:SUBLANES - 1],
                   jnp.where(row == 1, prev[SUBLANES - 1:SUBLANES], pltpu.roll(v, 2, 0)))
    carry_ref[...] = v[tm - SUBLANES:]
    cw = cw_ref[...]
    y = cw[0:1] * v2 + cw[1:2] * v1 + cw[2:3] * v
    o_conv = (gate_b * (y + cb_ref[...])).astype(BF16)

    gate_a = jax.nn.sigmoid(_dot(u, wmix_ref[:, n_conv:n_conv + D_MODEL]))
    gate_b = jax.nn.sigmoid(_dot(u, wmix_ref[:, n_conv + D_MODEL:]))
    merged = gate_a * _dot(on_ref[...], wa_ref[...]) + gate_b * _dot(o_conv, wb_ref[...])
    h2 = h1 + _dot(merged.astype(BF16), wo_ref[...])

    h3 = h2 + 0.5 * _swiglu_half(_rms(h2, n2_ref[...]).astype(BF16), wg_ref, wu_ref, wd_ref)

    gate_p = jax.nn.sigmoid(_dot(_rms(h3, np_ref[...]).astype(BF16), pg_ref[...]))
    h4 = h3 + gate_p * _dot(p_ref[...].astype(BF16), pp_ref[...])
    out_ref[...] = _rms(h4, nf_ref[...])


def _tail_call(h1, o_nsa, p2, seq, nm, wmix, cw, cb, wa, wb, wo, n2, wg, wu, wd, npl, pg, pp, nf):
    t = h1.shape[0]
    tm = TAIL_TM
    assert CONV_K - 1 <= SUBLANES and seq % tm == 0

    def row(width):
        return pl.BlockSpec((tm, width), lambda i: (i, 0))

    consts = [nm, wmix, cw, cb, wa, wb, wo, n2, wg, wu, wd, npl, pg, pp, nf]
    return pl.pallas_call(
        functools.partial(_tail_kernel, tiles_per_seq=seq // tm),
        grid=(t // tm,),
        in_specs=[row(D_MODEL), row(NSA_WIDTH), row(PLE_DIM)] + [_const_spec(c.shape) for c in consts],
        out_specs=row(D_MODEL),
        out_shape=jax.ShapeDtypeStruct((t, D_MODEL), F32),
        scratch_shapes=[pltpu.VMEM((SUBLANES, CONV_WIDTH), F32)],
        compiler_params=pltpu.CompilerParams(dimension_semantics=("arbitrary",),
                                             vmem_limit_bytes=VMEM_LIMIT),
        name="tail",
    )(h1, o_nsa, p2, *consts)


def _compress_params(pos, w1, b1, w2):
    eye = jnp.eye(NSA_KV_HEADS, dtype=F32)
    half = CMP_LEN // 2
    w1r = w1.reshape(CMP_LEN, HEAD_DIM, CMP_HIDDEN)

    def expand(w):
        return jnp.einsum('ldn,gh->lgdhn', w, eye).reshape(half * KV_WIDTH, NSA_KV_HEADS * CMP_HIDDEN).astype(BF16)

    def pos_row(pp):
        return jnp.broadcast_to(pp[:, None, :], (half, NSA_KV_HEADS, HEAD_DIM)).reshape(1, half * KV_WIDTH)

    w2p = jnp.pad(w2, ((0, 0), (0, LANES - HEAD_DIM)))
    w2b = jnp.einsum('nd,gh->gnhd', w2p, eye).reshape(NSA_KV_HEADS * CMP_HIDDEN, NSA_KV_HEADS * LANES).astype(BF16)
    b1t = jnp.tile(b1, NSA_KV_HEADS).reshape(1, NSA_KV_HEADS * CMP_HIDDEN)
    return (pos_row(pos[:half]), pos_row(pos[half:]), expand(w1r[:half]), expand(w1r[half:]), b1t, w2b)


def kernel(x, p, positions, ffn1_norm, ffn1_w_gate, ffn1_w_up, ffn1_w_down, mix_norm, w_in, cmp_k_pos, cmp_k_w1, cmp_k_b1, cmp_k_w2, cmp_v_pos, cmp_v_w1, cmp_v_b1, cmp_v_w2, conv_w, conv_b, w_proj_nsa, w_proj_conv, w_out, ffn2_norm, ffn2_w_gate, ffn2_w_up, ffn2_w_down, ple_norm, ple_w_gate, ple_w_proj, final_norm):
    b, s, d = x.shape
    t = b * s
    depth = ffn1_norm.shape[0]
    assert depth == 1 and d == D_MODEL and s % SEL_CHUNK == 0

    half = HEAD_DIM // 2
    inv_freq = ROPE_THETA ** (-jnp.arange(half, dtype=F32) / half)
    invf = jnp.tile(inv_freq, LANES // half).reshape(1, LANES)
    pos = positions.astype(F32).reshape(t // LANES, LANES)

    n_cmp_rows = s // CMP_STRIDE
    n_blk = s // SEL_LEN
    c_start = jnp.arange(n_cmp_rows) * CMP_STRIDE
    s_start = jnp.arange(n_blk) * SEL_LEN
    overlap_t = (jnp.clip(jnp.minimum(c_start[None, :] + CMP_LEN, s_start[:, None] + SEL_LEN)
                          - jnp.maximum(c_start[None, :], s_start[:, None]), 0, None).astype(F32)
                 / CMP_LEN).astype(BF16)

    row = lambda v: v.reshape(1, -1)
    h = x.reshape(t, d)
    for li in range(depth):
        wi = w_in[li]
        n_qkv = NSA_WIDTH + 6 * KV_WIDTH
        per_group = 3 * HPG
        n_gate = NSA_KV_HEADS * per_group
        wqkv = wi[:, :n_qkv]
        wgn = jnp.pad(wi[:, n_qkv:n_qkv + n_gate].reshape(d, NSA_KV_HEADS, per_group),
                      ((0, 0), (0, 0), (0, GATE_PAD - per_group))).reshape(d, NSA_KV_HEADS * GATE_PAD)
        wmix = wi[:, n_qkv + n_gate:]
        assert wmix.shape[1] == MIX_COLS

        h1, q, kc_r, vc_r, ks, vs, kw, vw, gates = _head_call(
            h, pos, invf, row(ffn1_norm[li]), ffn1_w_gate[li].astype(BF16), ffn1_w_up[li].astype(BF16),
            ffn1_w_down[li].astype(BF16), row(mix_norm[li]), wqkv.astype(BF16), wgn.astype(BF16), s)

        chunk = CMP_STRIDE * KV_WIDTH
        kc, vc = _compress_call(
            kc_r.reshape(b, n_cmp_rows, chunk), vc_r.reshape(b, n_cmp_rows, chunk),
            _compress_params(cmp_k_pos[li], cmp_k_w1[li], cmp_k_b1[li], cmp_k_w2[li]),
            _compress_params(cmp_v_pos[li], cmp_v_w1[li], cmp_v_b1[li], cmp_v_w2[li]))

        seq3 = lambda a: a.reshape(b, s, a.shape[-1])
        o_nsa = _attn_call(seq3(q), seq3(gates), kc, vc, seq3(ks), seq3(vs), seq3(kw), seq3(vw), overlap_t)

        h = _tail_call(
            h1, o_nsa.reshape(t, NSA_WIDTH), p[li].reshape(t, PLE_DIM), s,
            row(mix_norm[li]), wmix.astype(BF16),
            conv_w[li], row(conv_b[li]), w_proj_nsa[li].astype(BF16), w_proj_conv[li].astype(BF16),
            w_out[li].astype(BF16), row(ffn2_norm[li]), ffn2_w_gate[li].astype(BF16),
            ffn2_w_up[li].astype(BF16), ffn2_w_down[li].astype(BF16), row(ple_norm[li]),
            ple_w_gate[li].astype(BF16), ple_w_proj[li].astype(BF16), row(final_norm))
    return h.reshape(b, s, d)
```

```python
import functools
import math

import jax
import jax.numpy as jnp
from jax import lax
from jax.experimental import pallas as pl
from jax.experimental.pallas import tpu as pltpu

D_MODEL = 1024
D_FF = 2816
PLE_DIM = 256
EPS = 1e-6
ROPE_THETA = 10000.0

NSA_HEADS = 8
NSA_KV_HEADS = 2
HEAD_DIM = 64
HPG = NSA_HEADS // NSA_KV_HEADS
NSA_WIDTH = NSA_HEADS * HEAD_DIM
KV_WIDTH = NSA_KV_HEADS * HEAD_DIM
CMP_LEN = 32
CMP_STRIDE = 16
CMP_HIDDEN = 256
SEL_LEN = 64
N_SEL = 16
N_LOCAL = 2
WINDOW = 512
Q_BLOCK = 256
FORCED_SCORE = 1e4
INVALID_SCORE = -1e4
CONV_WIDTH = 512
CONV_K = 3
MIX_COLS = 3 * CONV_WIDTH + 2 * D_MODEL

SEL_SHIFT = SEL_LEN.bit_length() - 1

LANES = 128
SUBLANES = 8
VMEM_LIMIT = 58 * 1024 * 1024

HEAD_TM = 1024
TAIL_TM = 512
MXU_DIM = 256
FF_CHUNKS = (6 * MXU_DIM, 5 * MXU_DIM)
SEL_CHUNK = 512
SEL_SPAN_CHUNKS = 2
SEL_UNROLL = 2
GATE_PAD = LANES
BIAS_LANES = LANES - HEAD_DIM
ROPE_PACK = LANES // (HEAD_DIM // 2)
NEG = -(2.0 ** 100)
Q_SCALE = HEAD_DIM ** -0.5 * math.log2(math.e)

BF16 = jnp.bfloat16
F32 = jnp.float32


def _rms(x, g):
    return x * lax.rsqrt(jnp.mean(x * x, axis=-1, keepdims=True) + EPS) * g


def _dot(a, b):
    return jnp.dot(a, b, preferred_element_type=F32)


def _dot_nt(a, b):
    return lax.dot_general(a, b, (((1,), (1,)), ((), ())), preferred_element_type=F32)


def _swiglu_half(xn, wg_ref, wu_ref, wd_ref):
    acc = jnp.zeros((xn.shape[0], D_MODEL), F32)
    assert sum(FF_CHUNKS) == D_FF
    lo = 0
    for width in FF_CHUNKS:
        sl = slice(lo, lo + width)
        lo += width
        gate = _dot(xn, wg_ref[:, sl])
        up = _dot(xn, wu_ref[:, sl])
        act = (gate * jax.nn.sigmoid(gate) * up).astype(BF16)
        acc = acc + _dot(act, wd_ref[sl, :])
    return acc


def _head_kernel(x_ref, pos_ref, invf_ref, n1_ref, wg_ref, wu_ref, wd_ref, n2_ref,
                 wqkv_ref, wgn_ref,
                 h1_ref, q_ref, kc_ref, vc_ref, ks_ref, vs_ref, kw_ref, vw_ref, gate_ref, stage_ref, *, seq):
    x = x_ref[...]
    tm = x.shape[0]
    xn = _rms(x, n1_ref[...]).astype(BF16)

    def low_half(width):
        lane = lax.broadcasted_iota(jnp.int32, (tm, width), 1)
        return (lane & (HEAD_DIM - 1)) < (HEAD_DIM // 2)

    h1 = x + 0.5 * _swiglu_half(xn, wg_ref, wu_ref, wd_ref)
    h1_ref[...] = h1
    u = _rms(h1, n2_ref[...]).astype(BF16)

    half = HEAD_DIM // 2
    quarter = tm // ROPE_PACK
    qlane = lax.broadcasted_iota(jnp.int32, (quarter, LANES), 1)
    pos_rows = pos_ref[...]
    pos_cols = jnp.concatenate([pos_rows] * (LANES // pos_rows.shape[0]), axis=0).T
    cols_per_pack = quarter // LANES
    plane = lax.broadcasted_iota(jnp.int32, (LANES, LANES), 1)
    pieces = []
    for r in range(cols_per_pack):
        piece = jnp.broadcast_to(pos_cols[:, r:r + 1], (LANES, LANES))
        for p in range(1, ROPE_PACK):
            col = p * cols_per_pack + r
            piece = jnp.where(plane >= p * half, pos_cols[:, col:col + 1], piece)
        pieces.append(piece)
    packed_pos = jnp.concatenate(pieces, axis=0)
    ang = packed_pos * invf_ref[...]

    def unpack(t):
        out = []
        for p in range(ROPE_PACK):
            v = t if p == 0 else pltpu.roll(t, LANES - p * half, 1)
            span = half
            while span < LANES:
                v = jnp.where(qlane < span, v, pltpu.roll(v, span, 1))
                span *= 2
            out.append(v)
        return jnp.concatenate(out, axis=0)

    cos = unpack(jnp.cos(ang))
    sin = unpack(jnp.sin(ang))
    sin_signed = jnp.where(low_half(LANES), -sin, sin)

    def rope(z):
        width = z.shape[1]
        reps = width // LANES
        c = jnp.concatenate([cos] * reps, axis=1) if reps > 1 else cos
        s = jnp.concatenate([sin_signed] * reps, axis=1) if reps > 1 else sin_signed
        partner = jnp.where(low_half(width), pltpu.roll(z, width - HEAD_DIM // 2, 1),
                            pltpu.roll(z, HEAD_DIM // 2, 1))
        return z * c + partner * s

    zq = _dot(u, wqkv_ref[:, :NSA_WIDTH])
    q_ref[...] = (rope(zq) * Q_SCALE).astype(BF16)

    lane = lax.broadcasted_iota(jnp.int32, (tm, LANES), 1)
    is_dim = lane < HEAD_DIM
    ones_col = jnp.where(lane == HEAD_DIM, 1.0, 0.0)
    seq_pos = (pl.program_id(0) * tm + lax.broadcasted_iota(jnp.int32, (tm, LANES), 0)) & (seq - 1)
    blk_onehot = jnp.where(lane - HEAD_DIM == ((seq_pos >> SEL_SHIFT) & (BIAS_LANES - 1)), 1.0, 0.0)

    def per_group(z, extra):
        g0 = jnp.where(is_dim, z, extra)
        g1 = jnp.where(is_dim, pltpu.roll(z, HEAD_DIM, 1), extra)
        return jnp.concatenate([g0, g1], axis=1).astype(BF16)

    zkv = _dot(u, wqkv_ref[:, NSA_WIDTH:])
    stage_ref[0] = rope(zkv[:, 0 * KV_WIDTH:1 * KV_WIDTH])
    stage_ref[1] = zkv[:, 1 * KV_WIDTH:2 * KV_WIDTH]
    for which, dst_ref in enumerate((kc_ref, vc_ref)):
        for l in range(CMP_STRIDE):
            rows_l = stage_ref[which, pl.ds(l, tm // CMP_STRIDE, stride=CMP_STRIDE), :]
            dst_ref[:, l * KV_WIDTH:(l + 1) * KV_WIDTH] = rows_l.astype(BF16)
    ks_ref[...] = per_group(rope(zkv[:, 2 * KV_WIDTH:3 * KV_WIDTH]), blk_onehot)
    vs_ref[...] = per_group(zkv[:, 3 * KV_WIDTH:4 * KV_WIDTH], ones_col)
    kw_ref[...] = per_group(rope(zkv[:, 4 * KV_WIDTH:5 * KV_WIDTH]), 0.0)
    vw_ref[...] = per_group(zkv[:, 5 * KV_WIDTH:6 * KV_WIDTH], ones_col)
    gate_ref[...] = jax.nn.sigmoid(_dot(u, wgn_ref[...]))


def _const_spec(shape):
    nd = len(shape)
    return pl.BlockSpec(shape, lambda *_: (0,) * nd, pipeline_mode=pl.Buffered(1))


def _head_call(x2, pos, invf, n1, wg, wu, wd, n2, wqkv, wgn, seq):
    t = x2.shape[0]
    tm = HEAD_TM
    assert seq % tm == 0 and seq & (seq - 1) == 0 and tm % (SUBLANES * LANES) == 0

    def row(width):
        return pl.BlockSpec((tm, width), lambda i: (i, 0))

    def out(width, dtype):
        return jax.ShapeDtypeStruct((t, width), dtype)

    wide = NSA_KV_HEADS * LANES
    chunk = CMP_STRIDE * KV_WIDTH
    chunked = pl.BlockSpec((tm // CMP_STRIDE, chunk), lambda i: (i, 0))
    chunked_shape = jax.ShapeDtypeStruct((t // CMP_STRIDE, chunk), BF16)
    return pl.pallas_call(
        functools.partial(_head_kernel, seq=seq),
        grid=(t // tm,),
        in_specs=[row(D_MODEL), pl.BlockSpec((tm // LANES, LANES), lambda i: (i, 0)),
                  _const_spec((1, LANES)), _const_spec((1, D_MODEL)),
                  _const_spec((D_MODEL, D_FF)), _const_spec((D_MODEL, D_FF)), _const_spec((D_FF, D_MODEL)),
                  _const_spec((1, D_MODEL)), _const_spec((D_MODEL, NSA_WIDTH + 6 * KV_WIDTH)),
                  _const_spec((D_MODEL, NSA_KV_HEADS * GATE_PAD))],
        out_specs=[row(D_MODEL), row(NSA_WIDTH), chunked, chunked] + [row(wide)] * 4
                  + [row(NSA_KV_HEADS * GATE_PAD)],
        out_shape=[out(D_MODEL, F32), out(NSA_WIDTH, BF16), chunked_shape, chunked_shape]
                  + [out(wide, BF16)] * 4 + [out(NSA_KV_HEADS * GATE_PAD, F32)],
        scratch_shapes=[pltpu.VMEM((2, tm, KV_WIDTH), F32)],
        compiler_params=pltpu.CompilerParams(dimension_semantics=("arbitrary",),
                                             vmem_limit_bytes=VMEM_LIMIT),
        name="head",
    )(x2, pos, invf, n1, wg, wu, wd, n2, wqkv, wgn)


def _compress_kernel(kx_ref, vx_ref,
                     kpa_ref, kpb_ref, kwa_ref, kwb_ref, kb1_ref, kw2_ref,
                     vpa_ref, vpb_ref, vwa_ref, vwb_ref, vb1_ref, vw2_ref,
                     kc_ref, vc_ref):
    def one(x_ref, pa_ref, pb_ref, wa_ref, wb_ref, b1_ref, w2_ref):
        x = x_ref[0].astype(F32)
        first = _dot((x + pa_ref[...]).astype(BF16), wa_ref[...])
        second = _dot((x + pb_ref[...]).astype(BF16), wb_ref[...])
        n = first.shape[0]
        hidden = first + pltpu.roll(second, n - 1, 0) + b1_ref[...]
        return _dot(jax.nn.gelu(hidden).astype(BF16), w2_ref[...])

    kc_ref[0] = one(kx_ref, kpa_ref, kpb_ref, kwa_ref, kwb_ref, kb1_ref, kw2_ref).astype(BF16)
    vc_ref[0] = one(vx_ref, vpa_ref, vpb_ref, vwa_ref, vwb_ref, vb1_ref, vw2_ref).astype(BF16)


def _compress_call(kx, vx, kparams, vparams):
    b, nrow, width = kx.shape
    hid = NSA_KV_HEADS * CMP_HIDDEN
    wide = NSA_KV_HEADS * LANES
    x_spec = pl.BlockSpec((1, nrow, width), lambda i: (i, 0, 0))
    w_specs = [_const_spec((1, width)), _const_spec((1, width)), _const_spec((width, hid)),
               _const_spec((width, hid)), _const_spec((1, hid)), _const_spec((hid, wide))]
    o_spec = pl.BlockSpec((1, nrow, wide), lambda i: (i, 0, 0))
    o_shape = jax.ShapeDtypeStruct((b, nrow, wide), BF16)
    return pl.pallas_call(
        _compress_kernel,
        grid=(b,),
        in_specs=[x_spec, x_spec] + w_specs + w_specs,
        out_specs=[o_spec, o_spec],
        out_shape=[o_shape, o_shape],
        compiler_params=pltpu.CompilerParams(dimension_semantics=("arbitrary",),
                                             vmem_limit_bytes=VMEM_LIMIT),
        name="compress",
    )(kx, vx, *kparams, *vparams)


def _attn_kernel(q_ref, qn_ref, gate_ref, kc_ref, vc_ref, ks_ref, vs_ref, kw_ref, vw_ref, ovl_ref, o_ref,
                 qa_ref, oc_ref):
    qb = Q_BLOCK
    rows = HPG * qb
    groups = range(NSA_KV_HEADS)
    step = pl.program_id(1)
    slot = step % 2
    n_cmp = kc_ref.shape[1]
    n_blk = ovl_ref.shape[0]

    def tile(g):
        return slice(g * LANES, (g + 1) * LANES)

    lane = lax.broadcasted_iota(jnp.int32, (qb, LANES), 1)
    is_dim = lane < HEAD_DIM

    def positions(blk):
        return blk * qb + lax.broadcasted_iota(jnp.int32, (qb, 1), 0)

    def split_heads(block_ref):
        q_all = block_ref[0].astype(F32)
        q_heads = []
        for g in groups:
            heads = []
            for hh in range(HPG):
                pair = q_all[:, (g * HPG + hh) // 2 * LANES:((g * HPG + hh) // 2 + 1) * LANES]
                heads.append(pair if hh % 2 == 0 else pltpu.roll(pair, HEAD_DIM, 1))
            q_heads.append(heads)
        return q_heads

    def stack_heads(heads):
        return jnp.concatenate([jnp.where(is_dim, qh, 0.0) for qh in heads], axis=0).astype(BF16)

    def mask_heads(mask, s):
        s3 = s.reshape(HPG, qb, s.shape[-1])
        return jnp.where(mask[None], s3, NEG).reshape(rows, s.shape[-1])

    def normalise(acc):
        return acc * (1.0 / acc[:, HEAD_DIM:HEAD_DIM + 1])

    def select(block_ref, blk, dst):
        tq = positions(blk)
        q_heads = split_heads(block_ref)

        c_last = lax.broadcasted_iota(jnp.int32, (qb, n_cmp), 1) * CMP_STRIDE + (CMP_LEN - 1)
        c_mask = c_last <= tq
        s_c = [mask_heads(c_mask, _dot_nt(stack_heads(q_heads[g]), kc_ref[0, :, tile(g)])) for g in groups]
        sees_cmp = jnp.where(tq >= CMP_LEN - 1, 1.0, 0.0)[None]
        p_c = []
        for g in groups:
            e_c = jnp.exp2(s_c[g] - jnp.max(s_c[g], axis=-1, keepdims=True)).reshape(HPG, qb, n_cmp)
            p_c.append(e_c * (sees_cmp / jnp.sum(e_c, axis=-1, keepdims=True)))
            oc_ref[dst, g] = _dot(p_c[g].reshape(rows, n_cmp).astype(BF16), vc_ref[0, :, tile(g)])

        ovl = ovl_ref[...]
        imp = []
        for g in groups:
            p_sum = jnp.sum(p_c[g], axis=0)
            p_hi = p_sum.astype(BF16)
            p_lo = (p_sum - p_hi.astype(F32)).astype(BF16)
            imp.append(_dot_nt(ovl, p_hi) + _dot_nt(ovl, p_lo))
        imp = jnp.concatenate(imp, axis=1)
        nq = NSA_KV_HEADS * qb
        s_idx = lax.broadcasted_iota(jnp.int32, (n_blk, nq), 0)
        tq_row = blk * qb + (lax.broadcasted_iota(jnp.int32, (1, nq), 1) & (qb - 1))
        blk_q = tq_row >> SEL_SHIFT
        valid = s_idx <= blk_q
        forced = (s_idx == 0) | (valid & (s_idx > blk_q - N_LOCAL))
        score = jnp.where(forced, -jnp.inf, jnp.where(valid, imp, INVALID_SCORE))
        chosen = forced
        s_idx_f = s_idx.astype(F32)
        for _ in range(N_SEL - 1 - N_LOCAL):
            top = jnp.max(score, axis=0, keepdims=True)
            first = jnp.min(jnp.where(score == top, s_idx_f, float(n_blk)), axis=0, keepdims=True)
            hit = s_idx_f == first
            chosen = chosen | hit
            score = jnp.where(hit, -jnp.inf, score)
        bias = jnp.where(chosen & valid, 0.0, NEG)

        for g in groups:
            bias_g = bias[:, g * qb:(g + 1) * qb].T
            for w in range(n_blk // BIAS_LANES):
                b_w = bias_g[:, w * BIAS_LANES:(w + 1) * BIAS_LANES]
                b_w = jnp.concatenate([b_w, b_w], axis=1)
                qa_ref[dst, g, w] = jnp.concatenate([jnp.where(is_dim, qh, b_w) for qh in q_heads[g]],
                                                    axis=0).astype(BF16)

    @pl.when(step == 0)
    def _():
        select(q_ref, 0, 0)

    qs = step * qb
    tq = positions(step)
    ck = SEL_CHUNK
    span_keys = SEL_SPAN_CHUNKS * ck
    spans_per_bias = BIAS_LANES * SEL_LEN // span_keys
    own_chunk = (qs // ck) % SEL_SPAN_CHUNKS

    def sel_span(t, carry, n_spans, causal, which=groups):
        carry = list(carry)
        for g in which:
            m_prev, acc = carry[g]
            for u in range(n_spans):
                qa = qa_ref[slot, g, (n_spans * t + u) // spans_per_bias]
                scores = []
                for c in range(SEL_SPAN_CHUNKS):
                    c_eff = (own_chunk + c) % SEL_SPAN_CHUNKS if causal else c
                    start = pl.multiple_of((n_spans * t + u) * span_keys + c_eff * ck, ck)
                    s = _dot_nt(qa, ks_ref[0, pl.ds(start, ck), tile(g)])
                    if causal and c == 0:
                        s = mask_heads(start + lax.broadcasted_iota(jnp.int32, (qb, ck), 1) <= tq, s)
                    scores.append((s, start))
                for s, start in scores:
                    m_new = jnp.maximum(m_prev, jnp.max(s, axis=-1, keepdims=True))
                    pv = _dot(jnp.exp2(s - m_new).astype(BF16), vs_ref[0, pl.ds(start, ck), tile(g)])
                    m_prev, acc = m_new, jnp.exp2(m_prev - m_new) * acc + pv
            carry[g] = (m_prev, acc)
        return tuple(carry)

    n_plain = (qs + qb - 1) // span_keys
    carry = ((jnp.full((rows, 1), NEG, F32), jnp.zeros((rows, LANES), F32)),) * NSA_KV_HEADS
    n_wide = n_plain // SEL_UNROLL
    if SEL_UNROLL > 1:
        carry = lax.fori_loop(0, n_wide, functools.partial(sel_span, n_spans=SEL_UNROLL, causal=False), carry)
    carry = lax.fori_loop(n_wide * SEL_UNROLL, n_plain, functools.partial(sel_span, n_spans=1, causal=False),
                          carry)

    select(qn_ref, step + 1, 1 - slot)

    span = WINDOW + qb
    w_start = pl.multiple_of(jnp.maximum(qs - WINDOW, 0), qb)
    kpos_w = w_start + lax.broadcasted_iota(jnp.int32, (qb, span), 1)
    w_mask = (kpos_w <= tq) & (kpos_w > tq - WINDOW)
    q_heads = split_heads(q_ref)
    o_w = []
    for g in groups:
        s_w = mask_heads(w_mask, _dot_nt(stack_heads(q_heads[g]), kw_ref[0, pl.ds(w_start, span), tile(g)]))
        e_w = jnp.exp2(s_w - jnp.max(s_w, axis=-1, keepdims=True)).astype(BF16)
        o_w.append(normalise(_dot(e_w, vw_ref[0, pl.ds(w_start, span), tile(g)])))

    for g in groups:
        carry = sel_span(n_plain, carry, 1, True, which=(g,))
    o_s = [normalise(carry[g][1]) for g in groups]

    for g in groups:
        gates = gate_ref[0, :, tile(g)]
        o_c = oc_ref[slot, g]
        mixed = []
        for hh in range(HPG):
            sl = slice(hh * qb, (hh + 1) * qb)
            g_c = gates[:, 3 * hh + 0:3 * hh + 1]
            g_s = gates[:, 3 * hh + 1:3 * hh + 2]
            g_w = gates[:, 3 * hh + 2:3 * hh + 3]
            mixed.append(g_c * o_c[sl] + g_s * o_s[g][sl] + g_w * o_w[g][sl])
        for pr in range(HPG // 2):
            odd = pltpu.roll(mixed[2 * pr + 1], HEAD_DIM, 1)
            col = (g * HPG // 2 + pr) * LANES
            o_ref[0, :, col:col + LANES] = jnp.where(is_dim, mixed[2 * pr], odd).astype(BF16)


def _attn_call(q, gates, kc, vc, ks, vs, kw, vw, overlap_t):
    b, s, _ = q.shape
    n_cmp = kc.shape[1]
    n_blk = s // SEL_LEN
    span_keys = SEL_SPAN_CHUNKS * SEL_CHUNK
    assert n_blk % BIAS_LANES == 0 and BIAS_LANES * SEL_LEN % span_keys == 0 and s % span_keys == 0
    wide = NSA_KV_HEADS * LANES
    seq_spec = pl.BlockSpec((1, s, wide), lambda bi, i: (bi, 0, 0), pipeline_mode=pl.Buffered(1))
    cmp_spec = pl.BlockSpec((1, n_cmp, wide), lambda bi, i: (bi, 0, 0), pipeline_mode=pl.Buffered(1))
    n_steps = s // Q_BLOCK
    rows = HPG * Q_BLOCK
    return pl.pallas_call(
        _attn_kernel,
        grid=(b, n_steps),
        in_specs=[pl.BlockSpec((1, Q_BLOCK, NSA_WIDTH), lambda bi, i: (bi, i, 0)),
                  pl.BlockSpec((1, Q_BLOCK, NSA_WIDTH), lambda bi, i: (bi, jnp.minimum(i + 1, n_steps - 1), 0)),
                  pl.BlockSpec((1, Q_BLOCK, NSA_KV_HEADS * GATE_PAD), lambda bi, i: (bi, i, 0)),
                  cmp_spec, cmp_spec, seq_spec, seq_spec, seq_spec, seq_spec,
                  _const_spec(overlap_t.shape)],
        out_specs=pl.BlockSpec((1, Q_BLOCK, NSA_WIDTH), lambda bi, i: (bi, i, 0)),
        out_shape=jax.ShapeDtypeStruct((b, s, NSA_WIDTH), BF16),
        scratch_shapes=[pltpu.VMEM((2, NSA_KV_HEADS, n_blk // BIAS_LANES, rows, LANES), BF16),
                        pltpu.VMEM((2, NSA_KV_HEADS, rows, LANES), F32)],
        compiler_params=pltpu.CompilerParams(dimension_semantics=("arbitrary",) * 2,
                                             vmem_limit_bytes=VMEM_LIMIT),
        name="attention",
    )(q, q, gates, kc, vc, ks, vs, kw, vw, overlap_t)


def _tail_kernel(h1_ref, on_ref, p_ref, nm_ref, wmix_ref, cw_ref, cb_ref,
                 wa_ref, wb_ref, wo_ref, n2_ref, wg_ref, wu_ref, wd_ref,
                 np_ref, pg_ref, pp_ref, nf_ref, out_ref, carry_ref, *, tiles_per_seq):
    i = pl.program_id(0)
    h1 = h1_ref[...]
    tm = h1.shape[0]
    u = _rms(h1, nm_ref[...]).astype(BF16)

    n_conv = 3 * CONV_WIDTH
    zc = _dot(u, wmix_ref[:, :n_conv])
    gate_b = zc[:, :CONV_WIDTH]
    v = zc[:, CONV_WIDTH:2 * CONV_WIDTH] * zc[:, 2 * CONV_WIDTH:]

    @pl.when(i % tiles_per_seq == 0)
    def _():
        carry_ref[...] = jnp.zeros_like(carry_ref)

    prev = carry_ref[...]
    row = lax.broadcasted_iota(jnp.int32, (tm, CONV_WIDTH), 0)
    v1 = jnp.where(row == 0, prev[SUBLANES - 1:SUBLANES], pltpu.roll(v, 1, 0))
    v2 = jnp.where(row == 0, prev[SUBLANES - 2:SUBLANES - 1],
                   jnp.where(row == 1, prev[SUBLANES - 1:SUBLANES], pltpu.roll(v, 2, 0)))
    carry_ref[...] = v[tm - SUBLANES:]
    cw = cw_ref[...]
    y = cw[0:1] * v2 + cw[1:2] * v1 + cw[2:3] * v
    o_conv = (gate_b * (y + cb_ref[...])).astype(BF16)

    gate_a = jax.nn.sigmoid(_dot(u, wmix_ref[:, n_conv:n_conv + D_MODEL]))
    gate_b = jax.nn.sigmoid(_dot(u, wmix_ref[:, n_conv + D_MODEL:]))
    merged = gate_a * _dot(on_ref[...], wa_ref[...]) + gate_b * _dot(o_conv, wb_ref[...])
    h2 = h1 + _dot(merged.astype(BF16), wo_ref[...])

    h3 = h2 + 0.5 * _swiglu_half(_rms(h2, n2_ref[...]).astype(BF16), wg_ref, wu_ref, wd_ref)

    gate_p = jax.nn.sigmoid(_dot(_rms(h3, np_ref[...]).astype(BF16), pg_ref[...]))
    h4 = h3 + gate_p * _dot(p_ref[...].astype(BF16), pp_ref[...])
    out_ref[...] = _rms(h4, nf_ref[...])


def _tail_call(h1, o_nsa, p2, seq, nm, wmix, cw, cb, wa, wb, wo, n2, wg, wu, wd, npl, pg, pp, nf):
    t = h1.shape[0]
    tm = TAIL_TM
    assert CONV_K - 1 <= SUBLANES and seq % tm == 0

    def row(width):
        return pl.BlockSpec((tm, width), lambda i: (i, 0))

    consts = [nm, wmix, cw, cb, wa, wb, wo, n2, wg, wu, wd, npl, pg, pp, nf]
    return pl.pallas_call(
        functools.partial(_tail_kernel, tiles_per_seq=seq // tm),
        grid=(t // tm,),
        in_specs=[row(D_MODEL), row(NSA_WIDTH), row(PLE_DIM)] + [_const_spec(c.shape) for c in consts],
        out_specs=row(D_MODEL),
        out_shape=jax.ShapeDtypeStruct((t, D_MODEL), F32),
        scratch_shapes=[pltpu.VMEM((SUBLANES, CONV_WIDTH), F32)],
        compiler_params=pltpu.CompilerParams(dimension_semantics=("arbitrary",),
                                             vmem_limit_bytes=VMEM_LIMIT),
        name="tail",
    )(h1, o_nsa, p2, *consts)


def _compress_params(pos, w1, b1, w2):
    eye = jnp.eye(NSA_KV_HEADS, dtype=F32)
    half = CMP_LEN // 2
    w1r = w1.reshape(CMP_LEN, HEAD_DIM, CMP_HIDDEN)

    def expand(w):
        return jnp.einsum('ldn,gh->lgdhn', w, eye).reshape(half * KV_WIDTH, NSA_KV_HEADS * CMP_HIDDEN).astype(BF16)

    def pos_row(pp):
        return jnp.broadcast_to(pp[:, None, :], (half, NSA_KV_HEADS, HEAD_DIM)).reshape(1, half * KV_WIDTH)

    w2p = jnp.pad(w2, ((0, 0), (0, LANES - HEAD_DIM)))
    w2b = jnp.einsum('nd,gh->gnhd', w2p, eye).reshape(NSA_KV_HEADS * CMP_HIDDEN, NSA_KV_HEADS * LANES).astype(BF16)
    b1t = jnp.tile(b1, NSA_KV_HEADS).reshape(1, NSA_KV_HEADS * CMP_HIDDEN)
    return (pos_row(pos[:half]), pos_row(pos[half:]), expand(w1r[:half]), expand(w1r[half:]), b1t, w2b)


def kernel(x, p, positions, ffn1_norm, ffn1_w_gate, ffn1_w_up, ffn1_w_down, mix_norm, w_in, cmp_k_pos, cmp_k_w1, cmp_k_b1, cmp_k_w2, cmp_v_pos, cmp_v_w1, cmp_v_b1, cmp_v_w2, conv_w, conv_b, w_proj_nsa, w_proj_conv, w_out, ffn2_norm, ffn2_w_gate, ffn2_w_up, ffn2_w_down, ple_norm, ple_w_gate, ple_w_proj, final_norm):
    b, s, d = x.shape
    t = b * s
    depth = ffn1_norm.shape[0]
    assert depth == 1 and d == D_MODEL and s % SEL_CHUNK == 0

    half = HEAD_DIM // 2
    inv_freq = ROPE_THETA ** (-jnp.arange(half, dtype=F32) / half)
    invf = jnp.tile(inv_freq, LANES // half).reshape(1, LANES)
    pos = positions.astype(F32).reshape(t // LANES, LANES)

    n_cmp_rows = s // CMP_STRIDE
    n_blk = s // SEL_LEN
    c_start = jnp.arange(n_cmp_rows) * CMP_STRIDE
    s_start = jnp.arange(n_blk) * SEL_LEN
    overlap_t = (jnp.clip(jnp.minimum(c_start[None, :] + CMP_LEN, s_start[:, None] + SEL_LEN)
                          - jnp.maximum(c_start[None, :], s_start[:, None]), 0, None).astype(F32)
                 / CMP_LEN).astype(BF16)

    row = lambda v: v.reshape(1, -1)
    h = x.reshape(t, d)
    for li in range(depth):
        wi = w_in[li]
        n_qkv = NSA_WIDTH + 6 * KV_WIDTH
        per_group = 3 * HPG
        n_gate = NSA_KV_HEADS * per_group
        wqkv = wi[:, :n_qkv]
        wgn = jnp.pad(wi[:, n_qkv:n_qkv + n_gate].reshape(d, NSA_KV_HEADS, per_group),
                      ((0, 0), (0, 0), (0, GATE_PAD - per_group))).reshape(d, NSA_KV_HEADS * GATE_PAD)
        wmix = wi[:, n_qkv + n_gate:]
        assert wmix.shape[1] == MIX_COLS

        h1, q, kc_r, vc_r, ks, vs, kw, vw, gates = _head_call(
            h, pos, invf, row(ffn1_norm[li]), ffn1_w_gate[li].astype(BF16), ffn1_w_up[li].astype(BF16),
            ffn1_w_down[li].astype(BF16), row(mix_norm[li]), wqkv.astype(BF16), wgn.astype(BF16), s)

        chunk = CMP_STRIDE * KV_WIDTH
        kc, vc = _compress_call(
            kc_r.reshape(b, n_cmp_rows, chunk), vc_r.reshape(b, n_cmp_rows, chunk),
            _compress_params(cmp_k_pos[li], cmp_k_w1[li], cmp_k_b1[li], cmp_k_w2[li]),
            _compress_params(cmp_v_pos[li], cmp_v_w1[li], cmp_v_b1[li], cmp_v_w2[li]))

        seq3 = lambda a: a.reshape(b, s, a.shape[-1])
        o_nsa = _attn_call(seq3(q), seq3(gates), kc, vc, seq3(ks), seq3(vs), seq3(kw), seq3(vw), overlap_t)

        h = _tail_call(
            h1, o_nsa.reshape(t, NSA_WIDTH), p[li].reshape(t, PLE_DIM), s,
            row(mix_norm[li]), wmix.astype(BF16),
            conv_w[li], row(conv_b[li]), w_proj_nsa[li].astype(BF16), w_proj_conv[li].astype(BF16),
            w_out[li].astype(BF16), row(ffn2_norm[li]), ffn2_w_gate[li].astype(BF16),
            ffn2_w_up[li].astype(BF16), ffn2_w_down[li].astype(BF16), row(ple_norm[li]),
            ple_w_gate[li].astype(BF16), ple_w_proj[li].astype(BF16), row(final_norm))
    return h.reshape(b, s, d)
```

```python
import functools
import math

import jax
import jax.numpy as jnp
from jax import lax
from jax.experimental import pallas as pl
from jax.experimental.pallas import tpu as pltpu

D_MODEL = 1024
D_FF = 2816
PLE_DIM = 256
EPS = 1e-6
ROPE_THETA = 10000.0

NSA_HEADS = 8
NSA_KV_HEADS = 2
HEAD_DIM = 64
HPG = NSA_HEADS // NSA_KV_HEADS
NSA_WIDTH = NSA_HEADS * HEAD_DIM
KV_WIDTH = NSA_KV_HEADS * HEAD_DIM
CMP_LEN = 32
CMP_STRIDE = 16
CMP_HIDDEN = 256
SEL_LEN = 64
N_SEL = 16
N_LOCAL = 2
WINDOW = 512
Q_BLOCK = 128
FORCED_SCORE = 1e4
INVALID_SCORE = -1e4
CONV_WIDTH = 512
CONV_K = 3
MIX_COLS = 3 * CONV_WIDTH + 2 * D_MODEL

SEL_SHIFT = SEL_LEN.bit_length() - 1

LANES = 128
SUBLANES = 8
VMEM_LIMIT = 58 * 1024 * 1024

HEAD_TM = 1024
TAIL_TM = 512
MXU_DIM = 256
FF_CHUNKS = (6 * MXU_DIM, 5 * MXU_DIM)
SEL_CHUNK = 512
SEL_SPAN_CHUNKS = 2
SEL_UNROLL = 2
GATE_PAD = LANES
BIAS_LANES = LANES - HEAD_DIM
ROPE_PACK = LANES // (HEAD_DIM // 2)
NEG = -(2.0 ** 100)
Q_SCALE = HEAD_DIM ** -0.5 * math.log2(math.e)

BF16 = jnp.bfloat16
F32 = jnp.float32


def _rms(x, g):
    return x * lax.rsqrt(jnp.mean(x * x, axis=-1, keepdims=True) + EPS) * g


def _dot(a, b):
    return jnp.dot(a, b, preferred_element_type=F32)


def _dot_nt(a, b):
    return lax.dot_general(a, b, (((1,), (1,)), ((), ())), preferred_element_type=F32)


def _swiglu_half(xn, wg_ref, wu_ref, wd_ref):
    acc = jnp.zeros((xn.shape[0], D_MODEL), F32)
    assert sum(FF_CHUNKS) == D_FF
    lo = 0
    for width in FF_CHUNKS:
        sl = slice(lo, lo + width)
        lo += width
        gate = _dot(xn, wg_ref[:, sl])
        up = _dot(xn, wu_ref[:, sl])
        act = (gate * jax.nn.sigmoid(gate) * up).astype(BF16)
        acc = acc + _dot(act, wd_ref[sl, :])
    return acc


def _head_kernel(x_ref, pos_ref, invf_ref, n1_ref, wg_ref, wu_ref, wd_ref, n2_ref,
                 wqkv_ref, wgn_ref,
                 h1_ref, q_ref, kc_ref, vc_ref, ks_ref, vs_ref, kw_ref, vw_ref, gate_ref, stage_ref, *, seq):
    x = x_ref[...]
    tm = x.shape[0]
    xn = _rms(x, n1_ref[...]).astype(BF16)

    def low_half(width):
        lane = lax.broadcasted_iota(jnp.int32, (tm, width), 1)
        return (lane & (HEAD_DIM - 1)) < (HEAD_DIM // 2)

    h1 = x + 0.5 * _swiglu_half(xn, wg_ref, wu_ref, wd_ref)
    h1_ref[...] = h1
    u = _rms(h1, n2_ref[...]).astype(BF16)

    half = HEAD_DIM // 2
    quarter = tm // ROPE_PACK
    qlane = lax.broadcasted_iota(jnp.int32, (quarter, LANES), 1)
    pos_rows = pos_ref[...]
    pos_cols = jnp.concatenate([pos_rows] * (LANES // pos_rows.shape[0]), axis=0).T
    cols_per_pack = quarter // LANES
    plane = lax.broadcasted_iota(jnp.int32, (LANES, LANES), 1)
    pieces = []
    for r in range(cols_per_pack):
        piece = jnp.broadcast_to(pos_cols[:, r:r + 1], (LANES, LANES))
        for p in range(1, ROPE_PACK):
            col = p * cols_per_pack + r
            piece = jnp.where(plane >= p * half, pos_cols[:, col:col + 1], piece)
        pieces.append(piece)
    packed_pos = jnp.concatenate(pieces, axis=0)
    ang = packed_pos * invf_ref[...]

    def unpack(t):
        out = []
        for p in range(ROPE_PACK):
            v = t if p == 0 else pltpu.roll(t, LANES - p * half, 1)
            span = half
            while span < LANES:
                v = jnp.where(qlane < span, v, pltpu.roll(v, span, 1))
                span *= 2
            out.append(v)
        return jnp.concatenate(out, axis=0)

    cos = unpack(jnp.cos(ang))
    sin = unpack(jnp.sin(ang))
    sin_signed = jnp.where(low_half(LANES), -sin, sin)

    def rope(z):
        width = z.shape[1]
        reps = width // LANES
        c = jnp.concatenate([cos] * reps, axis=1) if reps > 1 else cos
        s = jnp.concatenate([sin_signed] * reps, axis=1) if reps > 1 else sin_signed
        partner = jnp.where(low_half(width), pltpu.roll(z, width - HEAD_DIM // 2, 1),
                            pltpu.roll(z, HEAD_DIM // 2, 1))
        return z * c + partner * s

    zq = _dot(u, wqkv_ref[:, :NSA_WIDTH])
    q_ref[...] = (rope(zq) * Q_SCALE).astype(BF16)

    lane = lax.broadcasted_iota(jnp.int32, (tm, LANES), 1)
    is_dim = lane < HEAD_DIM
    ones_col = jnp.where(lane == HEAD_DIM, 1.0, 0.0)
    seq_pos = (pl.program_id(0) * tm + lax.broadcasted_iota(jnp.int32, (tm, LANES), 0)) & (seq - 1)
    blk_onehot = jnp.where(lane - HEAD_DIM == ((seq_pos >> SEL_SHIFT) & (BIAS_LANES - 1)), 1.0, 0.0)

    def per_group(z, extra):
        g0 = jnp.where(is_dim, z, extra)
        g1 = jnp.where(is_dim, pltpu.roll(z, HEAD_DIM, 1), extra)
        return jnp.concatenate([g0, g1], axis=1).astype(BF16)

    zkv = _dot(u, wqkv_ref[:, NSA_WIDTH:])
    stage_ref[0] = rope(zkv[:, 0 * KV_WIDTH:1 * KV_WIDTH])
    stage_ref[1] = zkv[:, 1 * KV_WIDTH:2 * KV_WIDTH]
    for which, dst_ref in enumerate((kc_ref, vc_ref)):
        for l in range(CMP_STRIDE):
            rows_l = stage_ref[which, pl.ds(l, tm // CMP_STRIDE, stride=CMP_STRIDE), :]
            dst_ref[:, l * KV_WIDTH:(l + 1) * KV_WIDTH] = rows_l.astype(BF16)
    ks_ref[...] = per_group(rope(zkv[:, 2 * KV_WIDTH:3 * KV_WIDTH]), blk_onehot)
    vs_ref[...] = per_group(zkv[:, 3 * KV_WIDTH:4 * KV_WIDTH], ones_col)
    kw_ref[...] = per_group(rope(zkv[:, 4 * KV_WIDTH:5 * KV_WIDTH]), 0.0)
    vw_ref[...] = per_group(zkv[:, 5 * KV_WIDTH:6 * KV_WIDTH], ones_col)
    gate_ref[...] = jax.nn.sigmoid(_dot(u, wgn_ref[...]))


def _const_spec(shape):
    nd = len(shape)
    return pl.BlockSpec(shape, lambda *_: (0,) * nd, pipeline_mode=pl.Buffered(1))


def _head_call(x2, pos, invf, n1, wg, wu, wd, n2, wqkv, wgn, seq):
    t = x2.shape[0]
    tm = HEAD_TM
    assert seq % tm == 0 and seq & (seq - 1) == 0 and tm % (SUBLANES * LANES) == 0

    def row(width):
        return pl.BlockSpec((tm, width), lambda i: (i, 0))

    def out(width, dtype):
        return jax.ShapeDtypeStruct((t, width), dtype)

    wide = NSA_KV_HEADS * LANES
    chunk = CMP_STRIDE * KV_WIDTH
    chunked = pl.BlockSpec((tm // CMP_STRIDE, chunk), lambda i: (i, 0))
    chunked_shape = jax.ShapeDtypeStruct((t // CMP_STRIDE, chunk), BF16)
    return pl.pallas_call(
        functools.partial(_head_kernel, seq=seq),
        grid=(t // tm,),
        in_specs=[row(D_MODEL), pl.BlockSpec((tm // LANES, LANES), lambda i: (i, 0)),
                  _const_spec((1, LANES)), _const_spec((1, D_MODEL)),
                  _const_spec((D_MODEL, D_FF)), _const_spec((D_MODEL, D_FF)), _const_spec((D_FF, D_MODEL)),
                  _const_spec((1, D_MODEL)), _const_spec((D_MODEL, NSA_WIDTH + 6 * KV_WIDTH)),
                  _const_spec((D_MODEL, NSA_KV_HEADS * GATE_PAD))],
        out_specs=[row(D_MODEL), row(NSA_WIDTH), chunked, chunked] + [row(wide)] * 4
                  + [row(NSA_KV_HEADS * GATE_PAD)],
        out_shape=[out(D_MODEL, F32), out(NSA_WIDTH, BF16), chunked_shape, chunked_shape]
                  + [out(wide, BF16)] * 4 + [out(NSA_KV_HEADS * GATE_PAD, F32)],
        scratch_shapes=[pltpu.VMEM((2, tm, KV_WIDTH), F32)],
        compiler_params=pltpu.CompilerParams(dimension_semantics=("arbitrary",),
                                             vmem_limit_bytes=VMEM_LIMIT),
        name="head",
    )(x2, pos, invf, n1, wg, wu, wd, n2, wqkv, wgn)


def _compress_kernel(kx_ref, vx_ref,
                     kpa_ref, kpb_ref, kwa_ref, kwb_ref, kb1_ref, kw2_ref,
                     vpa_ref, vpb_ref, vwa_ref, vwb_ref, vb1_ref, vw2_ref,
                     kc_ref, vc_ref):
    def one(x_ref, pa_ref, pb_ref, wa_ref, wb_ref, b1_ref, w2_ref):
        x = x_ref[0].astype(F32)
        first = _dot((x + pa_ref[...]).astype(BF16), wa_ref[...])
        second = _dot((x + pb_ref[...]).astype(BF16), wb_ref[...])
        n = first.shape[0]
        hidden = first + pltpu.roll(second, n - 1, 0) + b1_ref[...]
        return _dot(jax.nn.gelu(hidden).astype(BF16), w2_ref[...])

    kc_ref[0] = one(kx_ref, kpa_ref, kpb_ref, kwa_ref, kwb_ref, kb1_ref, kw2_ref).astype(BF16)
    vc_ref[0] = one(vx_ref, vpa_ref, vpb_ref, vwa_ref, vwb_ref, vb1_ref, vw2_ref).astype(BF16)


def _compress_call(kx, vx, kparams, vparams):
    b, nrow, width = kx.shape
    hid = NSA_KV_HEADS * CMP_HIDDEN
    wide = NSA_KV_HEADS * LANES
    x_spec = pl.BlockSpec((1, nrow, width), lambda i: (i, 0, 0))
    w_specs = [_const_spec((1, width)), _const_spec((1, width)), _const_spec((width, hid)),
               _const_spec((width, hid)), _const_spec((1, hid)), _const_spec((hid, wide))]
    o_spec = pl.BlockSpec((1, nrow, wide), lambda i: (i, 0, 0))
    o_shape = jax.ShapeDtypeStruct((b, nrow, wide), BF16)
    return pl.pallas_call(
        _compress_kernel,
        grid=(b,),
        in_specs=[x_spec, x_spec] + w_specs + w_specs,
        out_specs=[o_spec, o_spec],
        out_shape=[o_shape, o_shape],
        compiler_params=pltpu.CompilerParams(dimension_semantics=("arbitrary",),
                                             vmem_limit_bytes=VMEM_LIMIT),
        name="compress",
    )(kx, vx, *kparams, *vparams)


def _attn_kernel(q_ref, qn_ref, gate_ref, kc_ref, vc_ref, ks_ref, vs_ref, kw_ref, vw_ref, ovl_ref, o_ref,
                 qa_ref, oc_ref):
    qb = Q_BLOCK
    rows = HPG * qb
    groups = range(NSA_KV_HEADS)
    step = pl.program_id(1)
    slot = step % 2
    n_cmp = kc_ref.shape[1]
    n_blk = ovl_ref.shape[0]

    def tile(g):
        return slice(g * LANES, (g + 1) * LANES)

    lane = lax.broadcasted_iota(jnp.int32, (qb, LANES), 1)
    is_dim = lane < HEAD_DIM

    def positions(blk):
        return blk * qb + lax.broadcasted_iota(jnp.int32, (qb, 1), 0)

    def split_heads(block_ref):
        q_all = block_ref[0].astype(F32)
        q_heads = []
        for g in groups:
            heads = []
            for hh in range(HPG):
                pair = q_all[:, (g * HPG + hh) // 2 * LANES:((g * HPG + hh) // 2 + 1) * LANES]
                heads.append(pair if hh % 2 == 0 else pltpu.roll(pair, HEAD_DIM, 1))
            q_heads.append(heads)
        return q_heads

    def stack_heads(heads):
        return jnp.concatenate([jnp.where(is_dim, qh, 0.0) for qh in heads], axis=0).astype(BF16)

    def mask_heads(mask, s):
        s3 = s.reshape(HPG, qb, s.shape[-1])
        return jnp.where(mask[None], s3, NEG).reshape(rows, s.shape[-1])

    def normalise(acc):
        return acc * (1.0 / acc[:, HEAD_DIM:HEAD_DIM + 1])

    def select(block_ref, blk, dst):
        tq = positions(blk)
        q_heads = split_heads(block_ref)

        c_last = lax.broadcasted_iota(jnp.int32, (qb, n_cmp), 1) * CMP_STRIDE + (CMP_LEN - 1)
        c_mask = c_last <= tq
        s_c = [mask_heads(c_mask, _dot_nt(stack_heads(q_heads[g]), kc_ref[0, :, tile(g)])) for g in groups]
        sees_cmp = jnp.where(tq >= CMP_LEN - 1, 1.0, 0.0)[None]
        p_c = []
        for g in groups:
            e_c = jnp.exp2(s_c[g] - jnp.max(s_c[g], axis=-1, keepdims=True)).reshape(HPG, qb, n_cmp)
            p_c.append(e_c * (sees_cmp / jnp.sum(e_c, axis=-1, keepdims=True)))
            oc_ref[dst, g] = _dot(p_c[g].reshape(rows, n_cmp).astype(BF16), vc_ref[0, :, tile(g)])

        ovl = ovl_ref[...]
        imp = []
        for g in groups:
            p_sum = jnp.sum(p_c[g], axis=0)
            p_hi = p_sum.astype(BF16)
            p_lo = (p_sum - p_hi.astype(F32)).astype(BF16)
            imp.append(_dot_nt(ovl, p_hi) + _dot_nt(ovl, p_lo))
        imp = jnp.concatenate(imp, axis=1)
        nq = NSA_KV_HEADS * qb
        s_idx = lax.broadcasted_iota(jnp.int32, (n_blk, nq), 0)
        tq_row = blk * qb + (lax.broadcasted_iota(jnp.int32, (1, nq), 1) & (qb - 1))
        blk_q = tq_row >> SEL_SHIFT
        valid = s_idx <= blk_q
        forced = (s_idx == 0) | (valid & (s_idx > blk_q - N_LOCAL))
        score = jnp.where(forced, -jnp.inf, jnp.where(valid, imp, INVALID_SCORE))
        chosen = forced
        s_idx_f = s_idx.astype(F32)
        for _ in range(N_SEL - 1 - N_LOCAL):
            top = jnp.max(score, axis=0, keepdims=True)
            first = jnp.min(jnp.where(score == top, s_idx_f, float(n_blk)), axis=0, keepdims=True)
            hit = s_idx_f == first
            chosen = chosen | hit
            score = jnp.where(hit, -jnp.inf, score)
        bias = jnp.where(chosen & valid, 0.0, NEG)

        for g in groups:
            bias_g = bias[:, g * qb:(g + 1) * qb].T
            for w in range(n_blk // BIAS_LANES):
                b_w = bias_g[:, w * BIAS_LANES:(w + 1) * BIAS_LANES]
                b_w = jnp.concatenate([b_w, b_w], axis=1)
                qa_ref[dst, g, w] = jnp.concatenate([jnp.where(is_dim, qh, b_w) for qh in q_heads[g]],
                                                    axis=0).astype(BF16)

    @pl.when(step == 0)
    def _():
        select(q_ref, 0, 0)

    qs = step * qb
    tq = positions(step)
    ck = SEL_CHUNK
    span_keys = SEL_SPAN_CHUNKS * ck
    spans_per_bias = BIAS_LANES * SEL_LEN // span_keys
    own_chunk = (qs // ck) % SEL_SPAN_CHUNKS

    def sel_span(t, carry, n_spans, causal, which=groups):
        carry = list(carry)
        for g in which:
            m_prev, acc = carry[g]
            for u in range(n_spans):
                qa = qa_ref[slot, g, (n_spans * t + u) // spans_per_bias]
                scores = []
                for c in range(SEL_SPAN_CHUNKS):
                    c_eff = (own_chunk + c) % SEL_SPAN_CHUNKS if causal else c
                    start = pl.multiple_of((n_spans * t + u) * span_keys + c_eff * ck, ck)
                    s = _dot_nt(qa, ks_ref[0, pl.ds(start, ck), tile(g)])
                    if causal and c == 0:
                        s = mask_heads(start + lax.broadcasted_iota(jnp.int32, (qb, ck), 1) <= tq, s)
                    scores.append((s, start))
                for s, start in scores:
                    m_new = jnp.maximum(m_prev, jnp.max(s, axis=-1, keepdims=True))
                    pv = _dot(jnp.exp2((s - m_new).astype(BF16)), vs_ref[0, pl.ds(start, ck), tile(g)])
                    m_prev, acc = m_new, jnp.exp2(m_prev - m_new) * acc + pv
            carry[g] = (m_prev, acc)
        return tuple(carry)

    n_plain = (qs + qb - 1) // span_keys
    carry = ((jnp.full((rows, 1), NEG, F32), jnp.zeros((rows, LANES), F32)),) * NSA_KV_HEADS
    n_wide = n_plain // SEL_UNROLL
    if SEL_UNROLL > 1:
        carry = lax.fori_loop(0, n_wide, functools.partial(sel_span, n_spans=SEL_UNROLL, causal=False), carry)
    carry = lax.fori_loop(n_wide * SEL_UNROLL, n_plain, functools.partial(sel_span, n_spans=1, causal=False),
                          carry)

    select(qn_ref, step + 1, 1 - slot)

    span = WINDOW + qb
    w_start = pl.multiple_of(jnp.maximum(qs - WINDOW, 0), qb)
    kpos_w = w_start + lax.broadcasted_iota(jnp.int32, (qb, span), 1)
    w_mask = (kpos_w <= tq) & (kpos_w > tq - WINDOW)
    q_heads = split_heads(q_ref)
    o_w = []
    for g in groups:
        s_w = mask_heads(w_mask, _dot_nt(stack_heads(q_heads[g]), kw_ref[0, pl.ds(w_start, span), tile(g)]))
        e_w = jnp.exp2((s_w - jnp.max(s_w, axis=-1, keepdims=True)).astype(BF16))
        o_w.append(normalise(_dot(e_w, vw_ref[0, pl.ds(w_start, span), tile(g)])))

    for g in groups:
        carry = sel_span(n_plain, carry, 1, True, which=(g,))
    o_s = [normalise(carry[g][1]) for g in groups]

    for g in groups:
        gates = gate_ref[0, :, tile(g)]
        o_c = oc_ref[slot, g]
        mixed = []
        for hh in range(HPG):
            sl = slice(hh * qb, (hh + 1) * qb)
            g_c = gates[:, 3 * hh + 0:3 * hh + 1]
            g_s = gates[:, 3 * hh + 1:3 * hh + 2]
            g_w = gates[:, 3 * hh + 2:3 * hh + 3]
            mixed.append(g_c * o_c[sl] + g_s * o_s[g][sl] + g_w * o_w[g][sl])
        for pr in range(HPG // 2):
            odd = pltpu.roll(mixed[2 * pr + 1], HEAD_DIM, 1)
            col = (g * HPG // 2 + pr) * LANES
            o_ref[0, :, col:col + LANES] = jnp.where(is_dim, mixed[2 * pr], odd).astype(BF16)


def _attn_call(q, gates, kc, vc, ks, vs, kw, vw, overlap_t):
    b, s, _ = q.shape
    n_cmp = kc.shape[1]
    n_blk = s // SEL_LEN
    span_keys = SEL_SPAN_CHUNKS * SEL_CHUNK
    assert n_blk % BIAS_LANES == 0 and BIAS_LANES * SEL_LEN % span_keys == 0 and s % span_keys == 0
    wide = NSA_KV_HEADS * LANES
    seq_spec = pl.BlockSpec((1, s, wide), lambda bi, i: (bi, 0, 0), pipeline_mode=pl.Buffered(1))
    cmp_spec = pl.BlockSpec((1, n_cmp, wide), lambda bi, i: (bi, 0, 0), pipeline_mode=pl.Buffered(1))
    n_steps = s // Q_BLOCK
    rows = HPG * Q_BLOCK
    return pl.pallas_call(
        _attn_kernel,
        grid=(b, n_steps),
        in_specs=[pl.BlockSpec((1, Q_BLOCK, NSA_WIDTH), lambda bi, i: (bi, i, 0)),
                  pl.BlockSpec((1, Q_BLOCK, NSA_WIDTH), lambda bi, i: (bi, jnp.minimum(i + 1, n_steps - 1), 0)),
                  pl.BlockSpec((1, Q_BLOCK, NSA_KV_HEADS * GATE_PAD), lambda bi, i: (bi, i, 0)),
                  cmp_spec, cmp_spec, seq_spec, seq_spec, seq_spec, seq_spec,
                  _const_spec(overlap_t.shape)],
        out_specs=pl.BlockSpec((1, Q_BLOCK, NSA_WIDTH), lambda bi, i: (bi, i, 0)),
        out_shape=jax.ShapeDtypeStruct((b, s, NSA_WIDTH), BF16),
        scratch_shapes=[pltpu.VMEM((2, NSA_KV_HEADS, n_blk // BIAS_LANES, rows, LANES), BF16),
                        pltpu.VMEM((2, NSA_KV_HEADS, rows, LANES), F32)],
        compiler_params=pltpu.CompilerParams(dimension_semantics=("arbitrary",) * 2,
                                             vmem_limit_bytes=VMEM_LIMIT),
        name="attention",
    )(q, q, gates, kc, vc, ks, vs, kw, vw, overlap_t)


def _tail_kernel(h1_ref, on_ref, p_ref, nm_ref, wmix_ref, cw_ref, cb_ref,
                 wa_ref, wb_ref, wo_ref, n2_ref, wg_ref, wu_ref, wd_ref,
                 np_ref, pg_ref, pp_ref, nf_ref, out_ref, carry_ref, *, tiles_per_seq):
    i = pl.program_id(0)
    h1 = h1_ref[...]
    tm = h1.shape[0]
    u = _rms(h1, nm_ref[...]).astype(BF16)

    n_conv = 3 * CONV_WIDTH
    zc = _dot(u, wmix_ref[:, :n_conv])
    gate_b = zc[:, :CONV_WIDTH]
    v = zc[:, CONV_WIDTH:2 * CONV_WIDTH] * zc[:, 2 * CONV_WIDTH:]

    @pl.when(i % tiles_per_seq == 0)
    def _():
        carry_ref[...] = jnp.zeros_like(carry_ref)

    prev = carry_ref[...]
    row = lax.broadcasted_iota(jnp.int32, (tm, CONV_WIDTH), 0)
    v1 = jnp.where(row == 0, prev[SUBLANES - 1:SUBLANES], pltpu.roll(v, 1, 0))
    v2 = jnp.where(row == 0, prev[SUBLANES - 2:SUBLANES - 1],
                   jnp.where(row == 1, prev[SUBLANES - 1:SUBLANES], pltpu.roll(v, 2, 0)))
    carry_ref[...] = v[tm - SUBLANES:]
    cw = cw_ref[...]
    y = cw[0:1] * v2 + cw[1:2] * v1 + cw[2:3] * v
    o_conv = (gate_b * (y + cb_ref[...])).astype(BF16)

    gate_a = jax.nn.sigmoid(_dot(u, wmix_ref[:, n_conv:n_conv + D_MODEL]))
    gate_b = jax.nn.sigmoid(_dot(u, wmix_ref[:, n_conv + D_MODEL:]))
    merged = gate_a * _dot(on_ref[...], wa_ref[...]) + gate_b * _dot(o_conv, wb_ref[...])
    h2 = h1 + _dot(merged.astype(BF16), wo_ref[...])

    h3 = h2 + 0.5 * _swiglu_half(_rms(h2, n2_ref[...]).astype(BF16), wg_ref, wu_ref, wd_ref)

    gate_p = jax.nn.sigmoid(_dot(_rms(h3, np_ref[...]).astype(BF16), pg_ref[...]))
    h4 = h3 + gate_p * _dot(p_ref[...].astype(BF16), pp_ref[...])
    out_ref[...] = _rms(h4, nf_ref[...])


def _tail_call(h1, o_nsa, p2, seq, nm, wmix, cw, cb, wa, wb, wo, n2, wg, wu, wd, npl, pg, pp, nf):
    t = h1.shape[0]
    tm = TAIL_TM
    assert CONV_K - 1 <= SUBLANES and seq % tm == 0

    def row(width):
        return pl.BlockSpec((tm, width), lambda i: (i, 0))

    consts = [nm, wmix, cw, cb, wa, wb, wo, n2, wg, wu, wd, npl, pg, pp, nf]
    return pl.pallas_call(
        functools.partial(_tail_kernel, tiles_per_seq=seq // tm),
        grid=(t // tm,),
        in_specs=[row(D_MODEL), row(NSA_WIDTH), row(PLE_DIM)] + [_const_spec(c.shape) for c in consts],
        out_specs=row(D_MODEL),
        out_shape=jax.ShapeDtypeStruct((t, D_MODEL), F32),
        scratch_shapes=[pltpu.VMEM((SUBLANES, CONV_WIDTH), F32)],
        compiler_params=pltpu.CompilerParams(dimension_semantics=("arbitrary",),
                                             vmem_limit_bytes=VMEM_LIMIT),
        name="tail",
    )(h1, o_nsa, p2, *consts)


def _compress_params(pos, w1, b1, w2):
    eye = jnp.eye(NSA_KV_HEADS, dtype=F32)
    half = CMP_LEN // 2
    w1r = w1.reshape(CMP_LEN, HEAD_DIM, CMP_HIDDEN)

    def expand(w):
        return jnp.einsum('ldn,gh->lgdhn', w, eye).reshape(half * KV_WIDTH, NSA_KV_HEADS * CMP_HIDDEN).astype(BF16)

    def pos_row(pp):
        return jnp.broadcast_to(pp[:, None, :], (half, NSA_KV_HEADS, HEAD_DIM)).reshape(1, half * KV_WIDTH)

    w2p = jnp.pad(w2, ((0, 0), (0, LANES - HEAD_DIM)))
    w2b = jnp.einsum('nd,gh->gnhd', w2p, eye).reshape(NSA_KV_HEADS * CMP_HIDDEN, NSA_KV_HEADS * LANES).astype(BF16)
    b1t = jnp.tile(b1, NSA_KV_HEADS).reshape(1, NSA_KV_HEADS * CMP_HIDDEN)
    return (pos_row(pos[:half]), pos_row(pos[half:]), expand(w1r[:half]), expand(w1r[half:]), b1t, w2b)


def kernel(x, p, positions, ffn1_norm, ffn1_w_gate, ffn1_w_up, ffn1_w_down, mix_norm, w_in, cmp_k_pos, cmp_k_w1, cmp_k_b1, cmp_k_w2, cmp_v_pos, cmp_v_w1, cmp_v_b1, cmp_v_w2, conv_w, conv_b, w_proj_nsa, w_proj_conv, w_out, ffn2_norm, ffn2_w_gate, ffn2_w_up, ffn2_w_down, ple_norm, ple_w_gate, ple_w_proj, final_norm):
    b, s, d = x.shape
    t = b * s
    depth = ffn1_norm.shape[0]
    assert depth == 1 and d == D_MODEL and s % SEL_CHUNK == 0

    half = HEAD_DIM // 2
    inv_freq = ROPE_THETA ** (-jnp.arange(half, dtype=F32) / half)
    invf = jnp.tile(inv_freq, LANES // half).reshape(1, LANES)
    pos = positions.astype(F32).reshape(t // LANES, LANES)

    n_cmp_rows = s // CMP_STRIDE
    n_blk = s // SEL_LEN
    c_start = jnp.arange(n_cmp_rows) * CMP_STRIDE
    s_start = jnp.arange(n_blk) * SEL_LEN
    overlap_t = (jnp.clip(jnp.minimum(c_start[None, :] + CMP_LEN, s_start[:, None] + SEL_LEN)
                          - jnp.maximum(c_start[None, :], s_start[:, None]), 0, None).astype(F32)
                 / CMP_LEN).astype(BF16)

    row = lambda v: v.reshape(1, -1)
    h = x.reshape(t, d)
    for li in range(depth):
        wi = w_in[li]
        n_qkv = NSA_WIDTH + 6 * KV_WIDTH
        per_group = 3 * HPG
        n_gate = NSA_KV_HEADS * per_group
        wqkv = wi[:, :n_qkv]
        wgn = jnp.pad(wi[:, n_qkv:n_qkv + n_gate].reshape(d, NSA_KV_HEADS, per_group),
                      ((0, 0), (0, 0), (0, GATE_PAD - per_group))).reshape(d, NSA_KV_HEADS * GATE_PAD)
        wmix = wi[:, n_qkv + n_gate:]
        assert wmix.shape[1] == MIX_COLS

        h1, q, kc_r, vc_r, ks, vs, kw, vw, gates = _head_call(
            h, pos, invf, row(ffn1_norm[li]), ffn1_w_gate[li].astype(BF16), ffn1_w_up[li].astype(BF16),
            ffn1_w_down[li].astype(BF16), row(mix_norm[li]), wqkv.astype(BF16), wgn.astype(BF16), s)

        chunk = CMP_STRIDE * KV_WIDTH
        kc, vc = _compress_call(
            kc_r.reshape(b, n_cmp_rows, chunk), vc_r.reshape(b, n_cmp_rows, chunk),
            _compress_params(cmp_k_pos[li], cmp_k_w1[li], cmp_k_b1[li], cmp_k_w2[li]),
            _compress_params(cmp_v_pos[li], cmp_v_w1[li], cmp_v_b1[li], cmp_v_w2[li]))

        seq3 = lambda a: a.reshape(b, s, a.shape[-1])
        o_nsa = _attn_call(seq3(q), seq3(gates), kc, vc, seq3(ks), seq3(vs), seq3(kw), seq3(vw), overlap_t)

        h = _tail_call(
            h1, o_nsa.reshape(t, NSA_WIDTH), p[li].reshape(t, PLE_DIM), s,
            row(mix_norm[li]), wmix.astype(BF16),
            conv_w[li], row(conv_b[li]), w_proj_nsa[li].astype(BF16), w_proj_conv[li].astype(BF16),
            w_out[li].astype(BF16), row(ffn2_norm[li]), ffn2_w_gate[li].astype(BF16),
            ffn2_w_up[li].astype(BF16), ffn2_w_down[li].astype(BF16), row(ple_norm[li]),
            ple_w_gate[li].astype(BF16), ple_w_proj[li].astype(BF16), row(final_norm))
    return h.reshape(b, s, d)
```

```python
import functools
import math

import jax
import jax.numpy as jnp
from jax import lax
from jax.experimental import pallas as pl
from jax.experimental.pallas import tpu as pltpu

D_MODEL = 1024
D_FF = 2816
PLE_DIM = 256
EPS = 1e-6
ROPE_THETA = 10000.0

NSA_HEADS = 8
NSA_KV_HEADS = 2
HEAD_DIM = 64
HPG = NSA_HEADS // NSA_KV_HEADS
NSA_WIDTH = NSA_HEADS * HEAD_DIM
KV_WIDTH = NSA_KV_HEADS * HEAD_DIM
CMP_LEN = 32
CMP_STRIDE = 16
CMP_HIDDEN = 256
SEL_LEN = 64
N_SEL = 16
N_LOCAL = 2
WINDOW = 512
Q_BLOCK = 128
FORCED_SCORE = 1e4
INVALID_SCORE = -1e4
CONV_WIDTH = 512
CONV_K = 3
MIX_COLS = 3 * CONV_WIDTH + 2 * D_MODEL

SEL_SHIFT = SEL_LEN.bit_length() - 1

LANES = 128
SUBLANES = 8
VMEM_LIMIT = 58 * 1024 * 1024

HEAD_TM = 1024
TAIL_TM = 512
MXU_DIM = 256
FF_CHUNKS = (6 * MXU_DIM, 5 * MXU_DIM)
SEL_CHUNK = 512
SEL_SPAN_CHUNKS = 2
SEL_UNROLL = 2
GATE_PAD = LANES
BIAS_LANES = LANES - HEAD_DIM
ROPE_PACK = LANES // (HEAD_DIM // 2)
NEG = -(2.0 ** 100)
Q_SCALE = HEAD_DIM ** -0.5 * math.log2(math.e)

BF16 = jnp.bfloat16
F32 = jnp.float32


def _rms(x, g):
    return x * lax.rsqrt(jnp.mean(x * x, axis=-1, keepdims=True) + EPS) * g


def _dot(a, b):
    return jnp.dot(a, b, preferred_element_type=F32)


def _dot_nt(a, b):
    return lax.dot_general(a, b, (((1,), (1,)), ((), ())), preferred_element_type=F32)


def _swiglu_half(xn, wg_ref, wu_ref, wd_ref):
    acc = jnp.zeros((xn.shape[0], D_MODEL), F32)
    assert sum(FF_CHUNKS) == D_FF
    lo = 0
    for width in FF_CHUNKS:
        sl = slice(lo, lo + width)
        lo += width
        gate = _dot(xn, wg_ref[:, sl])
        up = _dot(xn, wu_ref[:, sl])
        act = (gate * jax.nn.sigmoid(gate) * up).astype(BF16)
        acc = acc + _dot(act, wd_ref[sl, :])
    return acc


def _head_kernel(x_ref, pos_ref, invf_ref, n1_ref, wg_ref, wu_ref, wd_ref, n2_ref,
                 wqkv_ref, wgn_ref,
                 h1_ref, q_ref, kc_ref, vc_ref, ks_ref, vs_ref, kw_ref, vw_ref, gate_ref, stage_ref, *, seq):
    x = x_ref[...]
    tm = x.shape[0]
    xn = _rms(x, n1_ref[...]).astype(BF16)

    def low_half(width):
        lane = lax.broadcasted_iota(jnp.int32, (tm, width), 1)
        return (lane & (HEAD_DIM - 1)) < (HEAD_DIM // 2)

    h1 = x + 0.5 * _swiglu_half(xn, wg_ref, wu_ref, wd_ref)
    h1_ref[...] = h1
    u = _rms(h1, n2_ref[...]).astype(BF16)

    half = HEAD_DIM // 2
    quarter = tm // ROPE_PACK
    qlane = lax.broadcasted_iota(jnp.int32, (quarter, LANES), 1)
    pos_rows = pos_ref[...]
    pos_cols = jnp.concatenate([pos_rows] * (LANES // pos_rows.shape[0]), axis=0).T
    cols_per_pack = quarter // LANES
    plane = lax.broadcasted_iota(jnp.int32, (LANES, LANES), 1)
    pieces = []
    for r in range(cols_per_pack):
        piece = jnp.broadcast_to(pos_cols[:, r:r + 1], (LANES, LANES))
        for p in range(1, ROPE_PACK):
            col = p * cols_per_pack + r
            piece = jnp.where(plane >= p * half, pos_cols[:, col:col + 1], piece)
        pieces.append(piece)
    packed_pos = jnp.concatenate(pieces, axis=0)
    ang = packed_pos * invf_ref[...]

    def unpack(t):
        out = []
        for p in range(ROPE_PACK):
            v = t if p == 0 else pltpu.roll(t, LANES - p * half, 1)
            span = half
            while span < LANES:
                v = jnp.where(qlane < span, v, pltpu.roll(v, span, 1))
                span *= 2
            out.append(v)
        return jnp.concatenate(out, axis=0)

    cos = unpack(jnp.cos(ang))
    sin = unpack(jnp.sin(ang))
    sin_signed = jnp.where(low_half(LANES), -sin, sin)

    def rope(z):
        width = z.shape[1]
        reps = width // LANES
        c = jnp.concatenate([cos] * reps, axis=1) if reps > 1 else cos
        s = jnp.concatenate([sin_signed] * reps, axis=1) if reps > 1 else sin_signed
        partner = jnp.where(low_half(width), pltpu.roll(z, width - HEAD_DIM // 2, 1),
                            pltpu.roll(z, HEAD_DIM // 2, 1))
        return z * c + partner * s

    zq = _dot(u, wqkv_ref[:, :NSA_WIDTH])
    q_ref[...] = (rope(zq) * Q_SCALE).astype(BF16)

    lane = lax.broadcasted_iota(jnp.int32, (tm, LANES), 1)
    is_dim = lane < HEAD_DIM
    ones_col = jnp.where(lane == HEAD_DIM, 1.0, 0.0)
    seq_pos = (pl.program_id(0) * tm + lax.broadcasted_iota(jnp.int32, (tm, LANES), 0)) & (seq - 1)
    blk_onehot = jnp.where(lane - HEAD_DIM == ((seq_pos >> SEL_SHIFT) & (BIAS_LANES - 1)), 1.0, 0.0)

    def per_group(z, extra):
        g0 = jnp.where(is_dim, z, extra)
        g1 = jnp.where(is_dim, pltpu.roll(z, HEAD_DIM, 1), extra)
        return jnp.concatenate([g0, g1], axis=1).astype(BF16)

    zkv = _dot(u, wqkv_ref[:, NSA_WIDTH:])
    stage_ref[0] = rope(zkv[:, 0 * KV_WIDTH:1 * KV_WIDTH])
    stage_ref[1] = zkv[:, 1 * KV_WIDTH:2 * KV_WIDTH]
    for which, dst_ref in enumerate((kc_ref, vc_ref)):
        for l in range(CMP_STRIDE):
            rows_l = stage_ref[which, pl.ds(l, tm // CMP_STRIDE, stride=CMP_STRIDE), :]
            dst_ref[:, l * KV_WIDTH:(l + 1) * KV_WIDTH] = rows_l.astype(BF16)
    ks_ref[...] = per_group(rope(zkv[:, 2 * KV_WIDTH:3 * KV_WIDTH]), blk_onehot)
    vs_ref[...] = per_group(zkv[:, 3 * KV_WIDTH:4 * KV_WIDTH], ones_col)
    kw_ref[...] = per_group(rope(zkv[:, 4 * KV_WIDTH:5 * KV_WIDTH]), 0.0)
    vw_ref[...] = per_group(zkv[:, 5 * KV_WIDTH:6 * KV_WIDTH], ones_col)
    gate_ref[...] = jax.nn.sigmoid(_dot(u, wgn_ref[...]))


def _const_spec(shape):
    nd = len(shape)
    return pl.BlockSpec(shape, lambda *_: (0,) * nd, pipeline_mode=pl.Buffered(1))


def _head_call(x2, pos, invf, n1, wg, wu, wd, n2, wqkv, wgn, seq):
    t = x2.shape[0]
    tm = HEAD_TM
    assert seq % tm == 0 and seq & (seq - 1) == 0 and tm % (SUBLANES * LANES) == 0

    def row(width):
        return pl.BlockSpec((tm, width), lambda i: (i, 0))

    def out(width, dtype):
        return jax.ShapeDtypeStruct((t, width), dtype)

    wide = NSA_KV_HEADS * LANES
    chunk = CMP_STRIDE * KV_WIDTH
    chunked = pl.BlockSpec((tm // CMP_STRIDE, chunk), lambda i: (i, 0))
    chunked_shape = jax.ShapeDtypeStruct((t // CMP_STRIDE, chunk), BF16)
    return pl.pallas_call(
        functools.partial(_head_kernel, seq=seq),
        grid=(t // tm,),
        in_specs=[row(D_MODEL), pl.BlockSpec((tm // LANES, LANES), lambda i: (i, 0)),
                  _const_spec((1, LANES)), _const_spec((1, D_MODEL)),
                  _const_spec((D_MODEL, D_FF)), _const_spec((D_MODEL, D_FF)), _const_spec((D_FF, D_MODEL)),
                  _const_spec((1, D_MODEL)), _const_spec((D_MODEL, NSA_WIDTH + 6 * KV_WIDTH)),
                  _const_spec((D_MODEL, NSA_KV_HEADS * GATE_PAD))],
        out_specs=[row(D_MODEL), row(NSA_WIDTH), chunked, chunked] + [row(wide)] * 4
                  + [row(NSA_KV_HEADS * GATE_PAD)],
        out_shape=[out(D_MODEL, F32), out(NSA_WIDTH, BF16), chunked_shape, chunked_shape]
                  + [out(wide, BF16)] * 4 + [out(NSA_KV_HEADS * GATE_PAD, F32)],
        scratch_shapes=[pltpu.VMEM((2, tm, KV_WIDTH), F32)],
        compiler_params=pltpu.CompilerParams(dimension_semantics=("arbitrary",),
                                             vmem_limit_bytes=VMEM_LIMIT),
        name="head",
    )(x2, pos, invf, n1, wg, wu, wd, n2, wqkv, wgn)


def _compress_kernel(kx_ref, vx_ref,
                     kpa_ref, kpb_ref, kwa_ref, kwb_ref, kb1_ref, kw2_ref,
                     vpa_ref, vpb_ref, vwa_ref, vwb_ref, vb1_ref, vw2_ref,
                     kc_ref, vc_ref):
    def one(x_ref, pa_ref, pb_ref, wa_ref, wb_ref, b1_ref, w2_ref):
        x = x_ref[0].astype(F32)
        first = _dot((x + pa_ref[...]).astype(BF16), wa_ref[...])
        second = _dot((x + pb_ref[...]).astype(BF16), wb_ref[...])
        n = first.shape[0]
        hidden = first + pltpu.roll(second, n - 1, 0) + b1_ref[...]
        return _dot(jax.nn.gelu(hidden).astype(BF16), w2_ref[...])

    kc_ref[0] = one(kx_ref, kpa_ref, kpb_ref, kwa_ref, kwb_ref, kb1_ref, kw2_ref).astype(BF16)
    vc_ref[0] = one(vx_ref, vpa_ref, vpb_ref, vwa_ref, vwb_ref, vb1_ref, vw2_ref).astype(BF16)


def _compress_call(kx, vx, kparams, vparams):
    b, nrow, width = kx.shape
    hid = NSA_KV_HEADS * CMP_HIDDEN
    wide = NSA_KV_HEADS * LANES
    x_spec = pl.BlockSpec((1, nrow, width), lambda i: (i, 0, 0))
    w_specs = [_const_spec((1, width)), _const_spec((1, width)), _const_spec((width, hid)),
               _const_spec((width, hid)), _const_spec((1, hid)), _const_spec((hid, wide))]
    o_spec = pl.BlockSpec((1, nrow, wide), lambda i: (i, 0, 0))
    o_shape = jax.ShapeDtypeStruct((b, nrow, wide), BF16)
    return pl.pallas_call(
        _compress_kernel,
        grid=(b,),
        in_specs=[x_spec, x_spec] + w_specs + w_specs,
        out_specs=[o_spec, o_spec],
        out_shape=[o_shape, o_shape],
        compiler_params=pltpu.CompilerParams(dimension_semantics=("arbitrary",),
                                             vmem_limit_bytes=VMEM_LIMIT),
        name="compress",
    )(kx, vx, *kparams, *vparams)


def _attn_kernel(q_ref, qn_ref, gate_ref, win_ref, kc_ref, vc_ref, ks_ref, vs_ref, ovl_ref, o_ref,
                 qa_ref, oc_ref):
    qb = Q_BLOCK
    rows = HPG * qb
    groups = range(NSA_KV_HEADS)
    step = pl.program_id(1)
    slot = step % 2
    n_cmp = kc_ref.shape[1]
    n_blk = ovl_ref.shape[0]

    def tile(g):
        return slice(g * LANES, (g + 1) * LANES)

    lane = lax.broadcasted_iota(jnp.int32, (qb, LANES), 1)
    is_dim = lane < HEAD_DIM

    def positions(blk):
        return blk * qb + lax.broadcasted_iota(jnp.int32, (qb, 1), 0)

    def split_heads(block_ref):
        q_all = block_ref[0].astype(F32)
        q_heads = []
        for g in groups:
            heads = []
            for hh in range(HPG):
                pair = q_all[:, (g * HPG + hh) // 2 * LANES:((g * HPG + hh) // 2 + 1) * LANES]
                heads.append(pair if hh % 2 == 0 else pltpu.roll(pair, HEAD_DIM, 1))
            q_heads.append(heads)
        return q_heads

    def stack_heads(heads):
        return jnp.concatenate([jnp.where(is_dim, qh, 0.0) for qh in heads], axis=0).astype(BF16)

    def mask_heads(mask, s):
        s3 = s.reshape(HPG, qb, s.shape[-1])
        return jnp.where(mask[None], s3, NEG).reshape(rows, s.shape[-1])

    def normalise(acc):
        return acc * (1.0 / acc[:, HEAD_DIM:HEAD_DIM + 1])

    def select(block_ref, blk, dst):
        tq = positions(blk)
        q_heads = split_heads(block_ref)

        c_last = lax.broadcasted_iota(jnp.int32, (qb, n_cmp), 1) * CMP_STRIDE + (CMP_LEN - 1)
        c_mask = c_last <= tq
        s_c = [mask_heads(c_mask, _dot_nt(stack_heads(q_heads[g]), kc_ref[0, :, tile(g)])) for g in groups]
        sees_cmp = jnp.where(tq >= CMP_LEN - 1, 1.0, 0.0)[None]
        p_c = []
        for g in groups:
            e_c = jnp.exp2(s_c[g] - jnp.max(s_c[g], axis=-1, keepdims=True)).reshape(HPG, qb, n_cmp)
            p_c.append(e_c * (sees_cmp / jnp.sum(e_c, axis=-1, keepdims=True)))
            oc_ref[dst, g] = _dot(p_c[g].reshape(rows, n_cmp).astype(BF16), vc_ref[0, :, tile(g)])

        ovl = ovl_ref[...]
        imp = []
        for g in groups:
            p_sum = jnp.sum(p_c[g], axis=0)
            p_hi = p_sum.astype(BF16)
            p_lo = (p_sum - p_hi.astype(F32)).astype(BF16)
            imp.append(_dot_nt(ovl, p_hi) + _dot_nt(ovl, p_lo))
        imp = jnp.concatenate(imp, axis=1)
        nq = NSA_KV_HEADS * qb
        s_idx = lax.broadcasted_iota(jnp.int32, (n_blk, nq), 0)
        tq_row = blk * qb + (lax.broadcasted_iota(jnp.int32, (1, nq), 1) & (qb - 1))
        blk_q = tq_row >> SEL_SHIFT
        valid = s_idx <= blk_q
        forced = (s_idx == 0) | (valid & (s_idx > blk_q - N_LOCAL))
        score = jnp.where(forced, -jnp.inf, jnp.where(valid, imp, INVALID_SCORE))
        chosen = forced
        s_idx_f = s_idx.astype(F32)
        for _ in range(N_SEL - 1 - N_LOCAL):
            top = jnp.max(score, axis=0, keepdims=True)
            first = jnp.min(jnp.where(score == top, s_idx_f, float(n_blk)), axis=0, keepdims=True)
            hit = s_idx_f == first
            chosen = chosen | hit
            score = jnp.where(hit, -jnp.inf, score)
        bias = jnp.where(chosen & valid, 0.0, NEG)

        for g in groups:
            bias_g = bias[:, g * qb:(g + 1) * qb].T
            for w in range(n_blk // BIAS_LANES):
                b_w = bias_g[:, w * BIAS_LANES:(w + 1) * BIAS_LANES]
                b_w = jnp.concatenate([b_w, b_w], axis=1)
                qa_ref[dst, g, w] = jnp.concatenate([jnp.where(is_dim, qh, b_w) for qh in q_heads[g]],
                                                    axis=0).astype(BF16)

    @pl.when(step == 0)
    def _():
        select(q_ref, 0, 0)

    qs = step * qb
    tq = positions(step)
    ck = SEL_CHUNK
    span_keys = SEL_SPAN_CHUNKS * ck
    spans_per_bias = BIAS_LANES * SEL_LEN // span_keys
    own_chunk = (qs // ck) % SEL_SPAN_CHUNKS

    def sel_span(t, carry, n_spans, causal, which=groups):
        carry = list(carry)
        for g in which:
            m_prev, acc = carry[g]
            for u in range(n_spans):
                qa = qa_ref[slot, g, (n_spans * t + u) // spans_per_bias]
                scores = []
                for c in range(SEL_SPAN_CHUNKS):
                    c_eff = (own_chunk + c) % SEL_SPAN_CHUNKS if causal else c
                    start = pl.multiple_of((n_spans * t + u) * span_keys + c_eff * ck, ck)
                    s = _dot_nt(qa, ks_ref[0, pl.ds(start, ck), tile(g)])
                    if causal and c == 0:
                        s = mask_heads(start + lax.broadcasted_iota(jnp.int32, (qb, ck), 1) <= tq, s)
                    scores.append((s, start))
                for s, start in scores:
                    m_new = jnp.maximum(m_prev, jnp.max(s, axis=-1, keepdims=True))
                    pv = _dot(jnp.exp2(s - m_new).astype(BF16), vs_ref[0, pl.ds(start, ck), tile(g)])
                    m_prev, acc = m_new, jnp.exp2(m_prev - m_new) * acc + pv
            carry[g] = (m_prev, acc)
        return tuple(carry)

    n_plain = (qs + qb - 1) // span_keys
    carry = ((jnp.full((rows, 1), NEG, F32), jnp.zeros((rows, LANES), F32)),) * NSA_KV_HEADS
    n_wide = n_plain // SEL_UNROLL
    if SEL_UNROLL > 1:
        carry = lax.fori_loop(0, n_wide, functools.partial(sel_span, n_spans=SEL_UNROLL, causal=False), carry)
    carry = lax.fori_loop(n_wide * SEL_UNROLL, n_plain, functools.partial(sel_span, n_spans=1, causal=False),
                          carry)

    select(qn_ref, step + 1, 1 - slot)

    for g in groups:
        carry = sel_span(n_plain, carry, 1, True, which=(g,))
    o_s = [normalise(carry[g][1]) for g in groups]

    for g in groups:
        gates = gate_ref[0, :, tile(g)]
        o_c = oc_ref[slot, g]
        mixed = []
        for hh in range(HPG):
            sl = slice(hh * qb, (hh + 1) * qb)
            g_c = gates[:, 3 * hh + 0:3 * hh + 1]
            g_s = gates[:, 3 * hh + 1:3 * hh + 2]
            mixed.append(g_c * o_c[sl] + g_s * o_s[g][sl])
        for pr in range(HPG // 2):
            odd = pltpu.roll(mixed[2 * pr + 1], HEAD_DIM, 1)
            col = (g * HPG // 2 + pr) * LANES
            o_ref[0, :, col:col + LANES] = (jnp.where(is_dim, mixed[2 * pr], odd)
                                            + win_ref[0, :, col:col + LANES]).astype(BF16)


def _window_kernel(q_ref, gate_ref, kw_ref, vw_ref, o_ref):
    qb = Q_BLOCK
    rows = HPG * qb
    qs = pl.program_id(1) * qb
    lane = lax.broadcasted_iota(jnp.int32, (qb, LANES), 1)
    is_dim = lane < HEAD_DIM
    tq = qs + lax.broadcasted_iota(jnp.int32, (qb, 1), 0)
    span = WINDOW + qb
    w_start = pl.multiple_of(jnp.maximum(qs - WINDOW, 0), qb)
    kpos = w_start + lax.broadcasted_iota(jnp.int32, (qb, span), 1)
    w_mask = ((kpos <= tq) & (kpos > tq - WINDOW))[None]
    q_all = q_ref[0].astype(F32)
    for g in range(NSA_KV_HEADS):
        cols = slice(g * LANES, (g + 1) * LANES)
        heads = []
        for hh in range(HPG):
            pair = q_all[:, (g * HPG + hh) // 2 * LANES:((g * HPG + hh) // 2 + 1) * LANES]
            heads.append(pair if hh % 2 == 0 else pltpu.roll(pair, HEAD_DIM, 1))
        qg = jnp.concatenate([jnp.where(is_dim, qh, 0.0) for qh in heads], axis=0).astype(BF16)
        s = _dot_nt(qg, kw_ref[0, pl.ds(w_start, span), cols]).reshape(HPG, qb, span)
        s = jnp.where(w_mask, s, NEG).reshape(rows, span)
        e = jnp.exp2(s - jnp.max(s, axis=-1, keepdims=True)).astype(BF16)
        acc = _dot(e, vw_ref[0, pl.ds(w_start, span), cols])
        o_w = acc * (1.0 / acc[:, HEAD_DIM:HEAD_DIM + 1])
        gates = gate_ref[0, :, cols]
        mixed = [gates[:, 3 * hh + 2:3 * hh + 3] * o_w[hh * qb:(hh + 1) * qb] for hh in range(HPG)]
        for pr in range(HPG // 2):
            odd = pltpu.roll(mixed[2 * pr + 1], HEAD_DIM, 1)
            col = (g * HPG // 2 + pr) * LANES
            o_ref[0, :, col:col + LANES] = jnp.where(is_dim, mixed[2 * pr], odd)


def _window_call(q, gates, kw, vw):
    b, s, _ = q.shape
    wide = NSA_KV_HEADS * LANES
    seq_spec = pl.BlockSpec((1, s, wide), lambda bi, i: (bi, 0, 0), pipeline_mode=pl.Buffered(1))
    blk = lambda width: pl.BlockSpec((1, Q_BLOCK, width), lambda bi, i: (bi, i, 0))
    return pl.pallas_call(
        _window_kernel,
        grid=(b, s // Q_BLOCK),
        in_specs=[blk(NSA_WIDTH), blk(NSA_KV_HEADS * GATE_PAD), seq_spec, seq_spec],
        out_specs=blk(NSA_WIDTH),
        out_shape=jax.ShapeDtypeStruct((b, s, NSA_WIDTH), F32),
        compiler_params=pltpu.CompilerParams(dimension_semantics=("arbitrary",) * 2,
                                             vmem_limit_bytes=VMEM_LIMIT),
        name="window",
    )(q, gates, kw, vw)


def _attn_call(q, gates, win, kc, vc, ks, vs, overlap_t):
    b, s, _ = q.shape
    n_cmp = kc.shape[1]
    n_blk = s // SEL_LEN
    span_keys = SEL_SPAN_CHUNKS * SEL_CHUNK
    assert n_blk % BIAS_LANES == 0 and BIAS_LANES * SEL_LEN % span_keys == 0 and s % span_keys == 0
    wide = NSA_KV_HEADS * LANES
    seq_spec = pl.BlockSpec((1, s, wide), lambda bi, i: (bi, 0, 0), pipeline_mode=pl.Buffered(1))
    cmp_spec = pl.BlockSpec((1, n_cmp, wide), lambda bi, i: (bi, 0, 0), pipeline_mode=pl.Buffered(1))
    n_steps = s // Q_BLOCK
    rows = HPG * Q_BLOCK
    return pl.pallas_call(
        _attn_kernel,
        grid=(b, n_steps),
        in_specs=[pl.BlockSpec((1, Q_BLOCK, NSA_WIDTH), lambda bi, i: (bi, i, 0)),
                  pl.BlockSpec((1, Q_BLOCK, NSA_WIDTH), lambda bi, i: (bi, jnp.minimum(i + 1, n_steps - 1), 0)),
                  pl.BlockSpec((1, Q_BLOCK, NSA_KV_HEADS * GATE_PAD), lambda bi, i: (bi, i, 0)),
                  pl.BlockSpec((1, Q_BLOCK, NSA_WIDTH), lambda bi, i: (bi, i, 0)),
                  cmp_spec, cmp_spec, seq_spec, seq_spec,
                  _const_spec(overlap_t.shape)],
        out_specs=pl.BlockSpec((1, Q_BLOCK, NSA_WIDTH), lambda bi, i: (bi, i, 0)),
        out_shape=jax.ShapeDtypeStruct((b, s, NSA_WIDTH), BF16),
        scratch_shapes=[pltpu.VMEM((2, NSA_KV_HEADS, n_blk // BIAS_LANES, rows, LANES), BF16),
                        pltpu.VMEM((2, NSA_KV_HEADS, rows, LANES), F32)],
        compiler_params=pltpu.CompilerParams(dimension_semantics=("arbitrary",) * 2,
                                             vmem_limit_bytes=VMEM_LIMIT),
        name="attention",
    )(q, q, gates, win, kc, vc, ks, vs, overlap_t)


def _tail_kernel(h1_ref, on_ref, p_ref, nm_ref, wmix_ref, cw_ref, cb_ref,
                 wa_ref, wb_ref, wo_ref, n2_ref, wg_ref, wu_ref, wd_ref,
                 np_ref, pg_ref, pp_ref, nf_ref, out_ref, carry_ref, *, tiles_per_seq):
    i = pl.program_id(0)
    h1 = h1_ref[...]
    tm = h1.shape[0]
    u = _rms(h1, nm_ref[...]).astype(BF16)

    n_conv = 3 * CONV_WIDTH
    zc = _dot(u, wmix_ref[:, :n_conv])
    gate_b = zc[:, :CONV_WIDTH]
    v = zc[:, CONV_WIDTH:2 * CONV_WIDTH] * zc[:, 2 * CONV_WIDTH:]

    @pl.when(i % tiles_per_seq == 0)
    def _():
        carry_ref[...] = jnp.zeros_like(carry_ref)

    prev = carry_ref[...]
    row = lax.broadcasted_iota(jnp.int32, (tm, CONV_WIDTH), 0)
    v1 = jnp.where(row == 0, prev[SUBLANES - 1:SUBLANES], pltpu.roll(v, 1, 0))
    v2 = jnp.where(row == 0, prev[SUBLANES - 2:SUBLANES - 1],
                   jnp.where(row == 1, prev[SUBLANES - 1:SUBLANES], pltpu.roll(v, 2, 0)))
    carry_ref[...] = v[tm - SUBLANES:]
    cw = cw_ref[...]
    y = cw[0:1] * v2 + cw[1:2] * v1 + cw[2:3] * v
    o_conv = (gate_b * (y + cb_ref[...])).astype(BF16)

    gate_a = jax.nn.sigmoid(_dot(u, wmix_ref[:, n_conv:n_conv + D_MODEL]))
    gate_b = jax.nn.sigmoid(_dot(u, wmix_ref[:, n_conv + D_MODEL:]))
    merged = gate_a * _dot(on_ref[...], wa_ref[...]) + gate_b * _dot(o_conv, wb_ref[...])
    h2 = h1 + _dot(merged.astype(BF16), wo_ref[...])

    h3 = h2 + 0.5 * _swiglu_half(_rms(h2, n2_ref[...]).astype(BF16), wg_ref, wu_ref, wd_ref)

    gate_p = jax.nn.sigmoid(_dot(_rms(h3, np_ref[...]).astype(BF16), pg_ref[...]))
    h4 = h3 + gate_p * _dot(p_ref[...].astype(BF16), pp_ref[...])
    out_ref[...] = _rms(h4, nf_ref[...])


def _tail_call(h1, o_nsa, p2, seq, nm, wmix, cw, cb, wa, wb, wo, n2, wg, wu, wd, npl, pg, pp, nf):
    t = h1.shape[0]
    tm = TAIL_TM
    assert CONV_K - 1 <= SUBLANES and seq % tm == 0

    def row(width):
        return pl.BlockSpec((tm, width), lambda i: (i, 0))

    consts = [nm, wmix, cw, cb, wa, wb, wo, n2, wg, wu, wd, npl, pg, pp, nf]
    return pl.pallas_call(
        functools.partial(_tail_kernel, tiles_per_seq=seq // tm),
        grid=(t // tm,),
        in_specs=[row(D_MODEL), row(NSA_WIDTH), row(PLE_DIM)] + [_const_spec(c.shape) for c in consts],
        out_specs=row(D_MODEL),
        out_shape=jax.ShapeDtypeStruct((t, D_MODEL), F32),
        scratch_shapes=[pltpu.VMEM((SUBLANES, CONV_WIDTH), F32)],
        compiler_params=pltpu.CompilerParams(dimension_semantics=("arbitrary",),
                                             vmem_limit_bytes=VMEM_LIMIT),
        name="tail",
    )(h1, o_nsa, p2, *consts)


def _compress_params(pos, w1, b1, w2):
    eye = jnp.eye(NSA_KV_HEADS, dtype=F32)
    half = CMP_LEN // 2
    w1r = w1.reshape(CMP_LEN, HEAD_DIM, CMP_HIDDEN)

    def expand(w):
        return jnp.einsum('ldn,gh->lgdhn', w, eye).reshape(half * KV_WIDTH, NSA_KV_HEADS * CMP_HIDDEN).astype(BF16)

    def pos_row(pp):
        return jnp.broadcast_to(pp[:, None, :], (half, NSA_KV_HEADS, HEAD_DIM)).reshape(1, half * KV_WIDTH)

    w2p = jnp.pad(w2, ((0, 0), (0, LANES - HEAD_DIM)))
    w2b = jnp.einsum('nd,gh->gnhd', w2p, eye).reshape(NSA_KV_HEADS * CMP_HIDDEN, NSA_KV_HEADS * LANES).astype(BF16)
    b1t = jnp.tile(b1, NSA_KV_HEADS).reshape(1, NSA_KV_HEADS * CMP_HIDDEN)
    return (pos_row(pos[:half]), pos_row(pos[half:]), expand(w1r[:half]), expand(w1r[half:]), b1t, w2b)


def kernel(x, p, positions, ffn1_norm, ffn1_w_gate, ffn1_w_up, ffn1_w_down, mix_norm, w_in, cmp_k_pos, cmp_k_w1, cmp_k_b1, cmp_k_w2, cmp_v_pos, cmp_v_w1, cmp_v_b1, cmp_v_w2, conv_w, conv_b, w_proj_nsa, w_proj_conv, w_out, ffn2_norm, ffn2_w_gate, ffn2_w_up, ffn2_w_down, ple_norm, ple_w_gate, ple_w_proj, final_norm):
    b, s, d = x.shape
    t = b * s
    depth = ffn1_norm.shape[0]
    assert depth == 1 and d == D_MODEL and s % SEL_CHUNK == 0

    half = HEAD_DIM // 2
    inv_freq = ROPE_THETA ** (-jnp.arange(half, dtype=F32) / half)
    invf = jnp.tile(inv_freq, LANES // half).reshape(1, LANES)
    pos = positions.astype(F32).reshape(t // LANES, LANES)

    n_cmp_rows = s // CMP_STRIDE
    n_blk = s // SEL_LEN
    c_start = jnp.arange(n_cmp_rows) * CMP_STRIDE
    s_start = jnp.arange(n_blk) * SEL_LEN
    overlap_t = (jnp.clip(jnp.minimum(c_start[None, :] + CMP_LEN, s_start[:, None] + SEL_LEN)
                          - jnp.maximum(c_start[None, :], s_start[:, None]), 0, None).astype(F32)
                 / CMP_LEN).astype(BF16)

    row = lambda v: v.reshape(1, -1)
    h = x.reshape(t, d)
    for li in range(depth):
        wi = w_in[li]
        n_qkv = NSA_WIDTH + 6 * KV_WIDTH
        per_group = 3 * HPG
        n_gate = NSA_KV_HEADS * per_group
        wqkv = wi[:, :n_qkv]
        wgn = jnp.pad(wi[:, n_qkv:n_qkv + n_gate].reshape(d, NSA_KV_HEADS, per_group),
                      ((0, 0), (0, 0), (0, GATE_PAD - per_group))).reshape(d, NSA_KV_HEADS * GATE_PAD)
        wmix = wi[:, n_qkv + n_gate:]
        assert wmix.shape[1] == MIX_COLS

        h1, q, kc_r, vc_r, ks, vs, kw, vw, gates = _head_call(
            h, pos, invf, row(ffn1_norm[li]), ffn1_w_gate[li].astype(BF16), ffn1_w_up[li].astype(BF16),
            ffn1_w_down[li].astype(BF16), row(mix_norm[li]), wqkv.astype(BF16), wgn.astype(BF16), s)

        chunk = CMP_STRIDE * KV_WIDTH
        kc, vc = _compress_call(
            kc_r.reshape(b, n_cmp_rows, chunk), vc_r.reshape(b, n_cmp_rows, chunk),
            _compress_params(cmp_k_pos[li], cmp_k_w1[li], cmp_k_b1[li], cmp_k_w2[li]),
            _compress_params(cmp_v_pos[li], cmp_v_w1[li], cmp_v_b1[li], cmp_v_w2[li]))

        seq3 = lambda a: a.reshape(b, s, a.shape[-1])
        win = _window_call(seq3(q), seq3(gates), seq3(kw), seq3(vw))
        o_nsa = _attn_call(seq3(q), seq3(gates), win, kc, vc, seq3(ks), seq3(vs), overlap_t)

        h = _tail_call(
            h1, o_nsa.reshape(t, NSA_WIDTH), p[li].reshape(t, PLE_DIM), s,
            row(mix_norm[li]), wmix.astype(BF16),
            conv_w[li], row(conv_b[li]), w_proj_nsa[li].astype(BF16), w_proj_conv[li].astype(BF16),
            w_out[li].astype(BF16), row(ffn2_norm[li]), ffn2_w_gate[li].astype(BF16),
            ffn2_w_up[li].astype(BF16), ffn2_w_down[li].astype(BF16), row(ple_norm[li]),
            ple_w_gate[li].astype(BF16), ple_w_proj[li].astype(BF16), row(final_norm))
    return h.reshape(b, s, d)
```

```python
import functools
import math

import jax
import jax.numpy as jnp
from jax import lax
from jax.experimental import pallas as pl
from jax.experimental.pallas import tpu as pltpu

D_MODEL = 1024
D_FF = 2816
PLE_DIM = 256
EPS = 1e-6
ROPE_THETA = 10000.0

NSA_HEADS = 8
NSA_KV_HEADS = 2
HEAD_DIM = 64
HPG = NSA_HEADS // NSA_KV_HEADS
NSA_WIDTH = NSA_HEADS * HEAD_DIM
KV_WIDTH = NSA_KV_HEADS * HEAD_DIM
CMP_LEN = 32
CMP_STRIDE = 16
CMP_HIDDEN = 256
SEL_LEN = 64
N_SEL = 16
N_LOCAL = 2
WINDOW = 512
Q_BLOCK = 128
FORCED_SCORE = 1e4
INVALID_SCORE = -1e4
CONV_WIDTH = 512
CONV_K = 3
MIX_COLS = 3 * CONV_WIDTH + 2 * D_MODEL

SEL_SHIFT = SEL_LEN.bit_length() - 1

LANES = 128
SUBLANES = 8
VMEM_LIMIT = 58 * 1024 * 1024

HEAD_TM = 1024
TAIL_TM = 512
MXU_DIM = 256
FF_CHUNKS = (6 * MXU_DIM, 5 * MXU_DIM)
SEL_CHUNK = 512
SEL_SPAN_CHUNKS = 2
SEL_UNROLL = 2
GATE_PAD = LANES
BIAS_LANES = LANES - HEAD_DIM
ROPE_PACK = LANES // (HEAD_DIM // 2)
NEG = -(2.0 ** 100)
Q_SCALE = HEAD_DIM ** -0.5 * math.log2(math.e)

BF16 = jnp.bfloat16
F32 = jnp.float32


def _rms(x, g):
    return x * lax.rsqrt(jnp.mean(x * x, axis=-1, keepdims=True) + EPS) * g


def _dot(a, b):
    return jnp.dot(a, b, preferred_element_type=F32)


def _dot_nt(a, b):
    return lax.dot_general(a, b, (((1,), (1,)), ((), ())), preferred_element_type=F32)


def _swiglu_half(xn, wg_ref, wu_ref, wd_ref):
    acc = jnp.zeros((xn.shape[0], D_MODEL), F32)
    assert sum(FF_CHUNKS) == D_FF
    lo = 0
    for width in FF_CHUNKS:
        sl = slice(lo, lo + width)
        lo += width
        gate = _dot(xn, wg_ref[:, sl])
        up = _dot(xn, wu_ref[:, sl])
        act = (gate * jax.nn.sigmoid(gate) * up).astype(BF16)
        acc = acc + _dot(act, wd_ref[sl, :])
    return acc


def _head_kernel(x_ref, pos_ref, invf_ref, n1_ref, wg_ref, wu_ref, wd_ref, n2_ref,
                 wqkv_ref, wgn_ref,
                 h1_ref, q_ref, kc_ref, vc_ref, ks_ref, vs_ref, kw_ref, vw_ref, gate_ref, stage_ref, *, seq):
    x = x_ref[...]
    tm = x.shape[0]
    xn = _rms(x, n1_ref[...]).astype(BF16)

    def low_half(width):
        lane = lax.broadcasted_iota(jnp.int32, (tm, width), 1)
        return (lane & (HEAD_DIM - 1)) < (HEAD_DIM // 2)

    h1 = x + 0.5 * _swiglu_half(xn, wg_ref, wu_ref, wd_ref)
    h1_ref[...] = h1
    u = _rms(h1, n2_ref[...]).astype(BF16)

    half = HEAD_DIM // 2
    quarter = tm // ROPE_PACK
    qlane = lax.broadcasted_iota(jnp.int32, (quarter, LANES), 1)
    pos_rows = pos_ref[...]
    pos_cols = jnp.concatenate([pos_rows] * (LANES // pos_rows.shape[0]), axis=0).T
    cols_per_pack = quarter // LANES
    plane = lax.broadcasted_iota(jnp.int32, (LANES, LANES), 1)
    pieces = []
    for r in range(cols_per_pack):
        piece = jnp.broadcast_to(pos_cols[:, r:r + 1], (LANES, LANES))
        for p in range(1, ROPE_PACK):
            col = p * cols_per_pack + r
            piece = jnp.where(plane >= p * half, pos_cols[:, col:col + 1], piece)
        pieces.append(piece)
    packed_pos = jnp.concatenate(pieces, axis=0)
    ang = packed_pos * invf_ref[...]

    def unpack(t):
        out = []
        for p in range(ROPE_PACK):
            v = t if p == 0 else pltpu.roll(t, LANES - p * half, 1)
            span = half
            while span < LANES:
                v = jnp.where(qlane < span, v, pltpu.roll(v, span, 1))
                span *= 2
            out.append(v)
        return jnp.concatenate(out, axis=0)

    cos = unpack(jnp.cos(ang))
    sin = unpack(jnp.sin(ang))
    sin_signed = jnp.where(low_half(LANES), -sin, sin)

    def rope(z):
        width = z.shape[1]
        reps = width // LANES
        c = jnp.concatenate([cos] * reps, axis=1) if reps > 1 else cos
        s = jnp.concatenate([sin_signed] * reps, axis=1) if reps > 1 else sin_signed
        partner = jnp.where(low_half(width), pltpu.roll(z, width - HEAD_DIM // 2, 1),
                            pltpu.roll(z, HEAD_DIM // 2, 1))
        return z * c + partner * s

    zq = _dot(u, wqkv_ref[:, :NSA_WIDTH])
    q_ref[...] = (rope(zq) * Q_SCALE).astype(BF16)

    lane = lax.broadcasted_iota(jnp.int32, (tm, LANES), 1)
    is_dim = lane < HEAD_DIM
    ones_col = jnp.where(lane == HEAD_DIM, 1.0, 0.0)
    seq_pos = (pl.program_id(0) * tm + lax.broadcasted_iota(jnp.int32, (tm, LANES), 0)) & (seq - 1)
    blk_onehot = jnp.where(lane - HEAD_DIM == ((seq_pos >> SEL_SHIFT) & (BIAS_LANES - 1)), 1.0, 0.0)

    def per_group(z, extra):
        g0 = jnp.where(is_dim, z, extra)
        g1 = jnp.where(is_dim, pltpu.roll(z, HEAD_DIM, 1), extra)
        return jnp.concatenate([g0, g1], axis=1).astype(BF16)

    zkv = _dot(u, wqkv_ref[:, NSA_WIDTH:])
    stage_ref[0] = rope(zkv[:, 0 * KV_WIDTH:1 * KV_WIDTH])
    stage_ref[1] = zkv[:, 1 * KV_WIDTH:2 * KV_WIDTH]
    for which, dst_ref in enumerate((kc_ref, vc_ref)):
        for l in range(CMP_STRIDE):
            rows_l = stage_ref[which, pl.ds(l, tm // CMP_STRIDE, stride=CMP_STRIDE), :]
            dst_ref[:, l * KV_WIDTH:(l + 1) * KV_WIDTH] = rows_l.astype(BF16)
    ks_ref[...] = per_group(rope(zkv[:, 2 * KV_WIDTH:3 * KV_WIDTH]), blk_onehot)
    vs_ref[...] = per_group(zkv[:, 3 * KV_WIDTH:4 * KV_WIDTH], ones_col)
    kw_ref[...] = per_group(rope(zkv[:, 4 * KV_WIDTH:5 * KV_WIDTH]), 0.0)
    vw_ref[...] = per_group(zkv[:, 5 * KV_WIDTH:6 * KV_WIDTH], ones_col)
    gate_ref[...] = jax.nn.sigmoid(_dot(u, wgn_ref[...]))


def _const_spec(shape):
    nd = len(shape)
    return pl.BlockSpec(shape, lambda *_: (0,) * nd, pipeline_mode=pl.Buffered(1))


def _head_call(x2, pos, invf, n1, wg, wu, wd, n2, wqkv, wgn, seq):
    t = x2.shape[0]
    tm = HEAD_TM
    assert seq % tm == 0 and seq & (seq - 1) == 0 and tm % (SUBLANES * LANES) == 0

    def row(width):
        return pl.BlockSpec((tm, width), lambda i: (i, 0))

    def out(width, dtype):
        return jax.ShapeDtypeStruct((t, width), dtype)

    wide = NSA_KV_HEADS * LANES
    chunk = CMP_STRIDE * KV_WIDTH
    chunked = pl.BlockSpec((tm // CMP_STRIDE, chunk), lambda i: (i, 0))
    chunked_shape = jax.ShapeDtypeStruct((t // CMP_STRIDE, chunk), BF16)
    return pl.pallas_call(
        functools.partial(_head_kernel, seq=seq),
        grid=(t // tm,),
        in_specs=[row(D_MODEL), pl.BlockSpec((tm // LANES, LANES), lambda i: (i, 0)),
                  _const_spec((1, LANES)), _const_spec((1, D_MODEL)),
                  _const_spec((D_MODEL, D_FF)), _const_spec((D_MODEL, D_FF)), _const_spec((D_FF, D_MODEL)),
                  _const_spec((1, D_MODEL)), _const_spec((D_MODEL, NSA_WIDTH + 6 * KV_WIDTH)),
                  _const_spec((D_MODEL, NSA_KV_HEADS * GATE_PAD))],
        out_specs=[row(D_MODEL), row(NSA_WIDTH), chunked, chunked] + [row(wide)] * 4
                  + [row(NSA_KV_HEADS * GATE_PAD)],
        out_shape=[out(D_MODEL, F32), out(NSA_WIDTH, BF16), chunked_shape, chunked_shape]
                  + [out(wide, BF16)] * 4 + [out(NSA_KV_HEADS * GATE_PAD, F32)],
        scratch_shapes=[pltpu.VMEM((2, tm, KV_WIDTH), F32)],
        compiler_params=pltpu.CompilerParams(dimension_semantics=("arbitrary",),
                                             vmem_limit_bytes=VMEM_LIMIT),
        name="head",
    )(x2, pos, invf, n1, wg, wu, wd, n2, wqkv, wgn)


def _compress_kernel(kx_ref, vx_ref,
                     kpa_ref, kpb_ref, kwa_ref, kwb_ref, kb1_ref, kw2_ref,
                     vpa_ref, vpb_ref, vwa_ref, vwb_ref, vb1_ref, vw2_ref,
                     kc_ref, vc_ref):
    def one(x_ref, pa_ref, pb_ref, wa_ref, wb_ref, b1_ref, w2_ref):
        x = x_ref[0].astype(F32)
        first = _dot((x + pa_ref[...]).astype(BF16), wa_ref[...])
        second = _dot((x + pb_ref[...]).astype(BF16), wb_ref[...])
        n = first.shape[0]
        hidden = first + pltpu.roll(second, n - 1, 0) + b1_ref[...]
        return _dot(jax.nn.gelu(hidden).astype(BF16), w2_ref[...])

    kc_ref[0] = one(kx_ref, kpa_ref, kpb_ref, kwa_ref, kwb_ref, kb1_ref, kw2_ref).astype(BF16)
    vc_ref[0] = one(vx_ref, vpa_ref, vpb_ref, vwa_ref, vwb_ref, vb1_ref, vw2_ref).astype(BF16)


def _compress_call(kx, vx, kparams, vparams):
    b, nrow, width = kx.shape
    hid = NSA_KV_HEADS * CMP_HIDDEN
    wide = NSA_KV_HEADS * LANES
    x_spec = pl.BlockSpec((1, nrow, width), lambda i: (i, 0, 0))
    w_specs = [_const_spec((1, width)), _const_spec((1, width)), _const_spec((width, hid)),
               _const_spec((width, hid)), _const_spec((1, hid)), _const_spec((hid, wide))]
    o_spec = pl.BlockSpec((1, nrow, wide), lambda i: (i, 0, 0))
    o_shape = jax.ShapeDtypeStruct((b, nrow, wide), BF16)
    return pl.pallas_call(
        _compress_kernel,
        grid=(b,),
        in_specs=[x_spec, x_spec] + w_specs + w_specs,
        out_specs=[o_spec, o_spec],
        out_shape=[o_shape, o_shape],
        compiler_params=pltpu.CompilerParams(dimension_semantics=("arbitrary",),
                                             vmem_limit_bytes=VMEM_LIMIT),
        name="compress",
    )(kx, vx, *kparams, *vparams)


def _attn_kernel(q_ref, qn_ref, gate_ref, kc_ref, vc_ref, ks_ref, vs_ref, kw_ref, vw_ref, ovl_ref, o_ref,
                 qa_ref, oc_ref):
    qb = Q_BLOCK
    rows = HPG * qb
    groups = range(NSA_KV_HEADS)
    step = pl.program_id(1)
    slot = step % 2
    n_cmp = kc_ref.shape[1]
    n_blk = ovl_ref.shape[0]

    def tile(g):
        return slice(g * LANES, (g + 1) * LANES)

    lane = lax.broadcasted_iota(jnp.int32, (qb, LANES), 1)
    is_dim = lane < HEAD_DIM

    def positions(blk):
        return blk * qb + lax.broadcasted_iota(jnp.int32, (qb, 1), 0)

    def split_heads(block_ref):
        q_all = block_ref[0].astype(F32)
        q_heads = []
        for g in groups:
            heads = []
            for hh in range(HPG):
                pair = q_all[:, (g * HPG + hh) // 2 * LANES:((g * HPG + hh) // 2 + 1) * LANES]
                heads.append(pair if hh % 2 == 0 else pltpu.roll(pair, HEAD_DIM, 1))
            q_heads.append(heads)
        return q_heads

    def stack_heads(heads):
        return jnp.concatenate([jnp.where(is_dim, qh, 0.0) for qh in heads], axis=0).astype(BF16)

    def mask_heads(mask, s):
        s3 = s.reshape(HPG, qb, s.shape[-1])
        return jnp.where(mask[None], s3, NEG).reshape(rows, s.shape[-1])

    def normalise(acc):
        return acc * (1.0 / acc[:, HEAD_DIM:HEAD_DIM + 1])

    def select(block_ref, blk, dst):
        tq = positions(blk)
        q_heads = split_heads(block_ref)

        c_last = lax.broadcasted_iota(jnp.int32, (qb, n_cmp), 1) * CMP_STRIDE + (CMP_LEN - 1)
        c_mask = c_last <= tq
        s_c = [mask_heads(c_mask, _dot_nt(stack_heads(q_heads[g]), kc_ref[0, :, tile(g)])) for g in groups]
        sees_cmp = jnp.where(tq >= CMP_LEN - 1, 1.0, 0.0)[None]
        p_c = []
        for g in groups:
            e_c = jnp.exp2(s_c[g] - jnp.max(s_c[g], axis=-1, keepdims=True)).reshape(HPG, qb, n_cmp)
            p_c.append(e_c * (sees_cmp / jnp.sum(e_c, axis=-1, keepdims=True)))
            oc_ref[dst, g] = _dot(p_c[g].reshape(rows, n_cmp).astype(BF16), vc_ref[0, :, tile(g)])

        ovl = ovl_ref[...]
        imp = []
        for g in groups:
            p_sum = jnp.sum(p_c[g], axis=0)
            p_hi = p_sum.astype(BF16)
            p_lo = (p_sum - p_hi.astype(F32)).astype(BF16)
            imp.append(_dot_nt(ovl, p_hi) + _dot_nt(ovl, p_lo))
        imp = jnp.concatenate(imp, axis=1)
        nq = NSA_KV_HEADS * qb
        s_idx = lax.broadcasted_iota(jnp.int32, (n_blk, nq), 0)
        tq_row = blk * qb + (lax.broadcasted_iota(jnp.int32, (1, nq), 1) & (qb - 1))
        blk_q = tq_row >> SEL_SHIFT
        valid = s_idx <= blk_q
        forced = (s_idx == 0) | (valid & (s_idx > blk_q - N_LOCAL))
        score = jnp.where(forced, -jnp.inf, jnp.where(valid, imp, INVALID_SCORE))
        chosen = forced
        s_idx_f = s_idx.astype(F32)
        for _ in range(N_SEL - 1 - N_LOCAL):
            top = jnp.max(score, axis=0, keepdims=True)
            first = jnp.min(jnp.where(score == top, s_idx_f, float(n_blk)), axis=0, keepdims=True)
            hit = s_idx_f == first
            chosen = chosen | hit
            score = jnp.where(hit, -jnp.inf, score)
        bias = jnp.where(chosen & valid, 0.0, NEG)

        for g in groups:
            bias_g = bias[:, g * qb:(g + 1) * qb].T
            for w in range(n_blk // BIAS_LANES):
                b_w = bias_g[:, w * BIAS_LANES:(w + 1) * BIAS_LANES]
                b_w = jnp.concatenate([b_w, b_w], axis=1)
                qa_ref[dst, g, w] = jnp.concatenate([jnp.where(is_dim, qh, b_w) for qh in q_heads[g]],
                                                    axis=0).astype(BF16)

    @pl.when(step == 0)
    def _():
        select(q_ref, 0, 0)

    qs = step * qb
    tq = positions(step)
    ck = SEL_CHUNK
    span_keys = SEL_SPAN_CHUNKS * ck
    spans_per_bias = BIAS_LANES * SEL_LEN // span_keys
    own_chunk = (qs // ck) % SEL_SPAN_CHUNKS

    def sel_span(t, carry, n_spans, causal, which=groups):
        carry = list(carry)
        for g in which:
            m_prev, acc = carry[g]
            for u in range(n_spans):
                qa = qa_ref[slot, g, (n_spans * t + u) // spans_per_bias]
                scores = []
                for c in range(SEL_SPAN_CHUNKS):
                    c_eff = (own_chunk + c) % SEL_SPAN_CHUNKS if causal else c
                    start = pl.multiple_of((n_spans * t + u) * span_keys + c_eff * ck, ck)
                    s = _dot_nt(qa, ks_ref[0, pl.ds(start, ck), tile(g)])
                    if causal and c == 0:
                        s = mask_heads(start + lax.broadcasted_iota(jnp.int32, (qb, ck), 1) <= tq, s)
                    scores.append((s, start))
                for s, start in scores:
                    m_new = jnp.maximum(m_prev, jnp.max(s, axis=-1, keepdims=True))
                    pv = _dot(jnp.exp2(s - m_new).astype(BF16), vs_ref[0, pl.ds(start, ck), tile(g)])
                    m_prev, acc = m_new, jnp.exp2(m_prev - m_new) * acc + pv
            carry[g] = (m_prev, acc)
        return tuple(carry)

    n_plain = (qs + qb - 1) // span_keys
    carry = ((jnp.full((rows, 1), NEG, F32), jnp.zeros((rows, LANES), F32)),) * NSA_KV_HEADS
    n_wide = n_plain // SEL_UNROLL
    if SEL_UNROLL > 1:
        carry = lax.fori_loop(0, n_wide, functools.partial(sel_span, n_spans=SEL_UNROLL, causal=False), carry)
    carry = lax.fori_loop(n_wide * SEL_UNROLL, n_plain, functools.partial(sel_span, n_spans=1, causal=False),
                          carry)

    select(qn_ref, step + 1, 1 - slot)

    span = WINDOW + qb
    w_start = pl.multiple_of(jnp.maximum(qs - WINDOW, 0), qb)
    kpos_w = w_start + lax.broadcasted_iota(jnp.int32, (qb, span), 1)
    w_mask = (kpos_w <= tq) & (kpos_w > tq - WINDOW)
    q_heads = split_heads(q_ref)
    o_w = []
    for g in groups:
        s_w = mask_heads(w_mask, _dot_nt(stack_heads(q_heads[g]), kw_ref[0, pl.ds(w_start, span), tile(g)]))
        e_w = jnp.exp2(s_w - jnp.max(s_w, axis=-1, keepdims=True)).astype(BF16)
        o_w.append(normalise(_dot(e_w, vw_ref[0, pl.ds(w_start, span), tile(g)])))

    for g in groups:
        carry = sel_span(n_plain, carry, 1, True, which=(g,))
    o_s = [normalise(carry[g][1]) for g in groups]

    for g in groups:
        gates = gate_ref[0, :, tile(g)]
        o_c = oc_ref[slot, g]
        mixed = []
        for hh in range(HPG):
            sl = slice(hh * qb, (hh + 1) * qb)
            g_c = gates[:, 3 * hh + 0:3 * hh + 1]
            g_s = gates[:, 3 * hh + 1:3 * hh + 2]
            g_w = gates[:, 3 * hh + 2:3 * hh + 3]
            mixed.append(g_c * o_c[sl] + g_s * o_s[g][sl] + g_w * o_w[g][sl])
        for pr in range(HPG // 2):
            odd = pltpu.roll(mixed[2 * pr + 1], HEAD_DIM, 1)
            col = (g * HPG // 2 + pr) * LANES
            o_ref[0, :, col:col + LANES] = jnp.where(is_dim, mixed[2 * pr], odd).astype(BF16)


def _attn_call(q, gates, kc, vc, ks, vs, kw, vw, overlap_t):
    b, s, _ = q.shape
    n_cmp = kc.shape[1]
    n_blk = s // SEL_LEN
    span_keys = SEL_SPAN_CHUNKS * SEL_CHUNK
    assert n_blk % BIAS_LANES == 0 and BIAS_LANES * SEL_LEN % span_keys == 0 and s % span_keys == 0
    wide = NSA_KV_HEADS * LANES
    seq_spec = pl.BlockSpec((1, s, wide), lambda bi, i: (bi, 0, 0), pipeline_mode=pl.Buffered(1))
    cmp_spec = pl.BlockSpec((1, n_cmp, wide), lambda bi, i: (bi, 0, 0), pipeline_mode=pl.Buffered(1))
    n_steps = s // Q_BLOCK
    rows = HPG * Q_BLOCK
    return pl.pallas_call(
        _attn_kernel,
        grid=(b, n_steps),
        in_specs=[pl.BlockSpec((1, Q_BLOCK, NSA_WIDTH), lambda bi, i: (bi, i, 0)),
                  pl.BlockSpec((1, Q_BLOCK, NSA_WIDTH), lambda bi, i: (bi, jnp.minimum(i + 1, n_steps - 1), 0)),
                  pl.BlockSpec((1, Q_BLOCK, NSA_KV_HEADS * GATE_PAD), lambda bi, i: (bi, i, 0)),
                  cmp_spec, cmp_spec, seq_spec, seq_spec, seq_spec, seq_spec,
                  _const_spec(overlap_t.shape)],
        out_specs=pl.BlockSpec((1, Q_BLOCK, NSA_WIDTH), lambda bi, i: (bi, i, 0)),
        out_shape=jax.ShapeDtypeStruct((b, s, NSA_WIDTH), BF16),
        scratch_shapes=[pltpu.VMEM((2, NSA_KV_HEADS, n_blk // BIAS_LANES, rows, LANES), BF16),
                        pltpu.VMEM((2, NSA_KV_HEADS, rows, LANES), F32)],
        compiler_params=pltpu.CompilerParams(dimension_semantics=("arbitrary",) * 2,
                                             vmem_limit_bytes=VMEM_LIMIT),
        name="attention",
    )(q, q, gates, kc, vc, ks, vs, kw, vw, overlap_t)


def _tail_kernel(h1_ref, on_ref, p_ref, nm_ref, wmix_ref, cw_ref, cb_ref,
                 wa_ref, wb_ref, wo_ref, n2_ref, wg_ref, wu_ref, wd_ref,
                 np_ref, pg_ref, pp_ref, nf_ref, out_ref, carry_ref, *, tiles_per_seq):
    i = pl.program_id(0)
    h1 = h1_ref[...]
    tm = h1.shape[0]
    u = _rms(h1, nm_ref[...]).astype(BF16)

    n_conv = 3 * CONV_WIDTH
    zc = _dot(u, wmix_ref[:, :n_conv])
    gate_b = zc[:, :CONV_WIDTH]
    v = zc[:, CONV_WIDTH:2 * CONV_WIDTH] * zc[:, 2 * CONV_WIDTH:]

    @pl.when(i % tiles_per_seq == 0)
    def _():
        carry_ref[...] = jnp.zeros_like(carry_ref)

    prev = carry_ref[...]
    row = lax.broadcasted_iota(jnp.int32, (tm, CONV_WIDTH), 0)
    v1 = jnp.where(row == 0, prev[SUBLANES - 1:SUBLANES], pltpu.roll(v, 1, 0))
    v2 = jnp.where(row == 0, prev[SUBLANES - 2:SUBLANES - 1],
                   jnp.where(row == 1, prev[SUBLANES - 1:SUBLANES], pltpu.roll(v, 2, 0)))
    carry_ref[...] = v[tm - SUBLANES:]
    cw = cw_ref[...]
    y = cw[0:1] * v2 + cw[1:2] * v1 + cw[2:3] * v
    o_conv = (gate_b * (y + cb_ref[...])).astype(BF16)

    gate_a = jax.nn.sigmoid(_dot(u, wmix_ref[:, n_conv:n_conv + D_MODEL]))
    gate_b = jax.nn.sigmoid(_dot(u, wmix_ref[:, n_conv + D_MODEL:]))
    merged = gate_a * _dot(on_ref[...], wa_ref[...]) + gate_b * _dot(o_conv, wb_ref[...])
    h2 = h1 + _dot(merged.astype(BF16), wo_ref[...])

    h3 = h2 + 0.5 * _swiglu_half(_rms(h2, n2_ref[...]).astype(BF16), wg_ref, wu_ref, wd_ref)

    gate_p = jax.nn.sigmoid(_dot(_rms(h3, np_ref[...]).astype(BF16), pg_ref[...]))
    h4 = h3 + gate_p * _dot(p_ref[...].astype(BF16), pp_ref[...])
    out_ref[...] = _rms(h4, nf_ref[...])


def _tail_call(h1, o_nsa, p2, seq, nm, wmix, cw, cb, wa, wb, wo, n2, wg, wu, wd, npl, pg, pp, nf):
    t = h1.shape[0]
    tm = TAIL_TM
    assert CONV_K - 1 <= SUBLANES and seq % tm == 0

    def row(width):
        return pl.BlockSpec((tm, width), lambda i: (i, 0))

    consts = [nm, wmix, cw, cb, wa, wb, wo, n2, wg, wu, wd, npl, pg, pp, nf]
    return pl.pallas_call(
        functools.partial(_tail_kernel, tiles_per_seq=seq // tm),
        grid=(t // tm,),
        in_specs=[row(D_MODEL), row(NSA_WIDTH), row(PLE_DIM)] + [_const_spec(c.shape) for c in consts],
        out_specs=row(D_MODEL),
        out_shape=jax.ShapeDtypeStruct((t, D_MODEL), F32),
        scratch_shapes=[pltpu.VMEM((SUBLANES, CONV_WIDTH), F32)],
        compiler_params=pltpu.CompilerParams(dimension_semantics=("arbitrary",),
                                             vmem_limit_bytes=VMEM_LIMIT),
        name="tail",
    )(h1, o_nsa, p2, *consts)


def _compress_params(pos, w1, b1, w2):
    eye = jnp.eye(NSA_KV_HEADS, dtype=F32)
    half = CMP_LEN // 2
    w1r = w1.reshape(CMP_LEN, HEAD_DIM, CMP_HIDDEN)

    def expand(w):
        return jnp.einsum('ldn,gh->lgdhn', w, eye).reshape(half * KV_WIDTH, NSA_KV_HEADS * CMP_HIDDEN).astype(BF16)

    def pos_row(pp):
        return jnp.broadcast_to(pp[:, None, :], (half, NSA_KV_HEADS, HEAD_DIM)).reshape(1, half * KV_WIDTH)

    w2p = jnp.pad(w2, ((0, 0), (0, LANES - HEAD_DIM)))
    w2b = jnp.einsum('nd,gh->gnhd', w2p, eye).reshape(NSA_KV_HEADS * CMP_HIDDEN, NSA_KV_HEADS * LANES).astype(BF16)
    b1t = jnp.tile(b1, NSA_KV_HEADS).reshape(1, NSA_KV_HEADS * CMP_HIDDEN)
    return (pos_row(pos[:half]), pos_row(pos[half:]), expand(w1r[:half]), expand(w1r[half:]), b1t, w2b)


def kernel(x, p, positions, ffn1_norm, ffn1_w_gate, ffn1_w_up, ffn1_w_down, mix_norm, w_in, cmp_k_pos, cmp_k_w1, cmp_k_b1, cmp_k_w2, cmp_v_pos, cmp_v_w1, cmp_v_b1, cmp_v_w2, conv_w, conv_b, w_proj_nsa, w_proj_conv, w_out, ffn2_norm, ffn2_w_gate, ffn2_w_up, ffn2_w_down, ple_norm, ple_w_gate, ple_w_proj, final_norm):
    b, s, d = x.shape
    t = b * s
    depth = ffn1_norm.shape[0]
    assert depth == 1 and d == D_MODEL and s % SEL_CHUNK == 0

    half = HEAD_DIM // 2
    inv_freq = ROPE_THETA ** (-jnp.arange(half, dtype=F32) / half)
    invf = jnp.tile(inv_freq, LANES // half).reshape(1, LANES)
    pos = positions.astype(F32).reshape(t // LANES, LANES)

    n_cmp_rows = s // CMP_STRIDE
    n_blk = s // SEL_LEN
    c_start = jnp.arange(n_cmp_rows) * CMP_STRIDE
    s_start = jnp.arange(n_blk) * SEL_LEN
    overlap_t = (jnp.clip(jnp.minimum(c_start[None, :] + CMP_LEN, s_start[:, None] + SEL_LEN)
                          - jnp.maximum(c_start[None, :], s_start[:, None]), 0, None).astype(F32)
                 / CMP_LEN).astype(BF16)

    row = lambda v: v.reshape(1, -1)
    h = x.reshape(t, d)
    for li in range(depth):
        wi = w_in[li]
        n_qkv = NSA_WIDTH + 6 * KV_WIDTH
        per_group = 3 * HPG
        n_gate = NSA_KV_HEADS * per_group
        wqkv = wi[:, :n_qkv]
        wgn = jnp.pad(wi[:, n_qkv:n_qkv + n_gate].reshape(d, NSA_KV_HEADS, per_group),
                      ((0, 0), (0, 0), (0, GATE_PAD - per_group))).reshape(d, NSA_KV_HEADS * GATE_PAD)
        wmix = wi[:, n_qkv + n_gate:]
        assert wmix.shape[1] == MIX_COLS

        h1, q, kc_r, vc_r, ks, vs, kw, vw, gates = _head_call(
            h, pos, invf, row(ffn1_norm[li]), ffn1_w_gate[li].astype(BF16), ffn1_w_up[li].astype(BF16),
            ffn1_w_down[li].astype(BF16), row(mix_norm[li]), wqkv.astype(BF16), wgn.astype(BF16), s)

        chunk = CMP_STRIDE * KV_WIDTH
        fold = 2 if b % 2 == 0 else 1
        kc, vc = _compress_call(
            kc_r.reshape(b // fold, fold * n_cmp_rows, chunk), vc_r.reshape(b // fold, fold * n_cmp_rows, chunk),
            _compress_params(cmp_k_pos[li], cmp_k_w1[li], cmp_k_b1[li], cmp_k_w2[li]),
            _compress_params(cmp_v_pos[li], cmp_v_w1[li], cmp_v_b1[li], cmp_v_w2[li]))
        kc = kc.reshape(b, n_cmp_rows, kc.shape[-1])
        vc = vc.reshape(b, n_cmp_rows, vc.shape[-1])

        seq3 = lambda a: a.reshape(b, s, a.shape[-1])
        o_nsa = _attn_call(seq3(q), seq3(gates), kc, vc, seq3(ks), seq3(vs), seq3(kw), seq3(vw), overlap_t)

        h = _tail_call(
            h1, o_nsa.reshape(t, NSA_WIDTH), p[li].reshape(t, PLE_DIM), s,
            row(mix_norm[li]), wmix.astype(BF16),
            conv_w[li], row(conv_b[li]), w_proj_nsa[li].astype(BF16), w_proj_conv[li].astype(BF16),
            w_out[li].astype(BF16), row(ffn2_norm[li]), ffn2_w_gate[li].astype(BF16),
            ffn2_w_up[li].astype(BF16), ffn2_w_down[li].astype(BF16), row(ple_norm[li]),
            ple_w_gate[li].astype(BF16), ple_w_proj[li].astype(BF16), row(final_norm))
    return h.reshape(b, s, d)
```
